```python
import math
import jax, jax.numpy as jnp
from jax import lax
import numpy as np

D_MODEL = 1024
BATCH = 2
SEQ = 8192
DEPTH = 2

DA_HEADS = 4
DA_QK_DIM = 64
DA_V_DIM = 2 * DA_QK_DIM
DA_ROT_DIM = DA_QK_DIM // 4
ROPE_THETA = 500000.0
Q_BLOCK = 128
RET_HEADS = 4
RET_QK_DIM = 32
RET_V_DIM = 2 * RET_QK_DIM
RET_THETA = 10000.0
RET_CHUNK = 128
POOL_GROUPS = 4
POOL_DIM = 64
POOL_WINDOWS = (2, 4, 8, 16)
DA_WIDTH = DA_HEADS * DA_V_DIM
RET_WIDTH = RET_HEADS * RET_V_DIM
POOL_WIDTH = POOL_GROUPS * POOL_DIM
MIX_WIDTH = DA_WIDTH + RET_WIDTH + POOL_WIDTH
SPLIT_SIZES = (DA_HEADS * 2 * DA_QK_DIM, DA_HEADS * 2 * DA_QK_DIM, DA_WIDTH,
               RET_HEADS * RET_QK_DIM, RET_HEADS * RET_QK_DIM, RET_WIDTH, RET_WIDTH,
               POOL_WIDTH)
IN_WIDTH = 2560
D_FF = 2816
CONV_WIDTH = 3
EPS = 1e-6

kernel_name = "hymba_style_diffattn_retention_pool_hybrid"


def rms_norm(x, g):
    xf = x.astype(jnp.float32)
    y = xf * lax.rsqrt(jnp.mean(xf * xf, axis=-1, keepdims=True) + EPS)
    return (y * g).astype(x.dtype)


def rotary(x, pos, rot_dim, theta):
    inv = jnp.float32(theta) ** (-jnp.arange(0, rot_dim, 2, dtype=jnp.float32) / rot_dim)
    ang = pos[:, None] * inv[None, :]
    cos = jnp.cos(ang).astype(x.dtype)
    sin = jnp.sin(ang).astype(x.dtype)
    half = rot_dim // 2
    x1, x2, xp = x[..., :half], x[..., half:rot_dim], x[..., rot_dim:]
    return jnp.concatenate([x1 * cos - x2 * sin, x2 * cos + x1 * sin, xp], axis=-1)


def diff_attention(q, k, v, lq1, lk1, lq2, lk2, subln_g, lam_init):
    B, S, _ = q.shape
    pos = jnp.arange(S, dtype=jnp.float32)
    q = q.reshape(B, S, DA_HEADS, 2, DA_QK_DIM).transpose(0, 2, 3, 1, 4)
    k = k.reshape(B, S, DA_HEADS, 2, DA_QK_DIM).transpose(0, 2, 3, 1, 4)
    q = rotary(q, pos, DA_ROT_DIM, ROPE_THETA)
    k = rotary(k, pos, DA_ROT_DIM, ROPE_THETA)
    v = v.reshape(B, S, DA_HEADS, DA_V_DIM).transpose(0, 2, 1, 3)
    lam = (jnp.exp(jnp.sum(lq1.astype(jnp.float32) * lk1.astype(jnp.float32)))
           - jnp.exp(jnp.sum(lq2.astype(jnp.float32) * lk2.astype(jnp.float32))) + lam_init)
    scale = DA_QK_DIM ** -0.5
    nb = S // Q_BLOCK
    qb = q.reshape(B, DA_HEADS, 2, nb, Q_BLOCK, DA_QK_DIM).transpose(3, 0, 1, 2, 4, 5)
    kpos = jnp.arange(S)

    def block(args):
        q_blk, i = args
        s = jnp.einsum('bhmqd,bhmkd->bhmqk', q_blk, k).astype(jnp.float32) * scale
        qpos = i * Q_BLOCK + jnp.arange(Q_BLOCK)
        mask = kpos[None, :] <= qpos[:, None]
        p = jax.nn.softmax(jnp.where(mask, s, -jnp.inf), axis=-1)
        a = p[:, :, 0] - lam * p[:, :, 1]
        return jnp.einsum('bhqk,bhkd->bhqd', a.astype(v.dtype), v)

    o = lax.map(block, (qb, jnp.arange(nb)))
    o = o.transpose(1, 0, 3, 2, 4).reshape(B, S, DA_HEADS, DA_V_DIM)
    o = rms_norm(o, subln_g) * (1.0 - lam_init)
    return o.reshape(B, S, DA_WIDTH)


def retention(q, k, v, g, ret_g):
    B, S, _ = q.shape
    H, dk, dv, C = RET_HEADS, RET_QK_DIM, RET_V_DIM, RET_CHUNK
    pos = jnp.arange(S, dtype=jnp.float32)
    q = rotary(q.reshape(B, S, H, dk).transpose(0, 2, 1, 3), pos, dk, RET_THETA)
    k = rotary(k.reshape(B, S, H, dk).transpose(0, 2, 1, 3), pos, dk, RET_THETA) * (dk ** -0.5)
    v = v.reshape(B, S, H, dv).transpose(0, 2, 1, 3)
    log_g = jnp.log(1.0 - 2.0 ** (-5.0 - jnp.arange(H, dtype=jnp.float32)))
    idx = jnp.arange(C, dtype=jnp.float32)
    diff = idx[:, None] - idx[None, :]
    decay = jnp.where(diff >= 0, jnp.exp(jnp.maximum(diff, 0.0) * log_g[:, None, None]), 0.0)
    xi = jnp.exp((idx + 1.0) * log_g[:, None])[..., None]
    zeta = jnp.exp((C - 1.0 - idx) * log_g[:, None])[..., None]
    chunk_decay = jnp.exp(C * log_g)[:, None, None]
    nc = S // C

    def to_chunks(t):
        return t.reshape(B, H, nc, C, t.shape[-1]).transpose(2, 0, 1, 3, 4).astype(jnp.float32)

    def step(state, inp):
        qi, ki, vi = inp
        inner = jnp.einsum('bhqd,bhkd->bhqk', qi, ki) * decay
        o = (jnp.einsum('bhqk,bhkv->bhqv', inner, vi)
             + jnp.einsum('bhqd,bhdv->bhqv', qi * xi, state))
        state = state * chunk_decay + jnp.einsum('bhkd,bhkv->bhdv', ki * zeta, vi)
        return state, o

    state0 = jnp.zeros((B, H, dk, dv), jnp.float32)
    _, o = lax.scan(step, state0, (to_chunks(q), to_chunks(k), to_chunks(v)))
    o = o.transpose(1, 0, 3, 2, 4).reshape(B, S, H, dv)
    mu = jnp.mean(o, axis=-1, keepdims=True)
    var = jnp.mean(jnp.square(o - mu), axis=-1, keepdims=True)
    o = (o - mu) * lax.rsqrt(var + EPS) * ret_g.reshape(H, dv)
    return jax.nn.silu(g) * o.reshape(B, S, RET_WIDTH).astype(g.dtype)


def pool_mixer(u, pool_w, pool_scale):
    B, S, _ = u.shape
    ug = u.reshape(B, S, POOL_GROUPS, POOL_DIM).astype(jnp.float32)
    c = jnp.cumsum(ug, axis=1)
    t = jnp.arange(S, dtype=jnp.float32)
    means = []
    for gi, w in enumerate(POOL_WINDOWS):
        cg = c[:, :, gi]
        shifted = jnp.pad(cg, ((0, 0), (w, 0), (0, 0)))[:, :S]
        means.append((cg - shifted) / jnp.minimum(t + 1.0, float(w))[:, None])
    pooled = (jnp.stack(means, axis=2) - ug).astype(u.dtype)
    y = jnp.einsum('bsgp,gpq->bsgq', pooled, pool_w)
    return y.reshape(B, S, POOL_WIDTH) * pool_scale


def conv_mlp(h, w_up, conv_w, conv_b, w_down):
    S = h.shape[1]
    u = h @ w_up
    up = jnp.pad(u, ((0, 0), (CONV_WIDTH - 1, 0), (0, 0)))
    uc = conv_b + sum(conv_w[j] * up[:, j:j + S] for j in range(CONV_WIDTH))
    gate, val = jnp.split(uc, 2, axis=-1)
    return (jax.nn.gelu(gate, approximate=True) * val) @ w_down


def setup_inputs(seed: int = 0) -> dict:
    key = jax.random.key(seed)
    ks = jax.random.split(key, 24)
    f32 = jnp.float32

    def nrm(k, shape, scale):
        return jax.random.normal(k, shape, f32) * scale

    def gain(k, shape):
        return 1.0 + 0.05 * jax.random.normal(k, shape, f32)

    L = DEPTH
    return {
        "x": nrm(ks[0], (BATCH, SEQ, D_MODEL), 1.0),
        "norm_mix_pre": gain(ks[1], (L, D_MODEL)),
        "norm_mix_post": gain(ks[2], (L, D_MODEL)),
        "w_in": nrm(ks[3], (L, D_MODEL, IN_WIDTH), D_MODEL ** -0.5),
        "lambda_q1": nrm(ks[4], (L, DA_QK_DIM), 0.1),
        "lambda_k1": nrm(ks[5], (L, DA_QK_DIM), 0.1),
        "lambda_q2": nrm(ks[6], (L, DA_QK_DIM), 0.1),
        "lambda_k2": nrm(ks[7], (L, DA_QK_DIM), 0.1),
        "diff_subln": gain(ks[8], (L, DA_V_DIM)),
        "ret_norm": gain(ks[9], (L, RET_WIDTH)),
        "pool_w": nrm(ks[10], (L, POOL_GROUPS, POOL_DIM, POOL_DIM), POOL_DIM ** -0.5),
        "pool_scale": 0.5 + 0.1 * jax.random.normal(ks[11], (L, POOL_WIDTH), f32),
        "w_out": nrm(ks[12], (L, MIX_WIDTH, D_MODEL), MIX_WIDTH ** -0.5),
        "norm_mlp_pre": gain(ks[13], (L, D_MODEL)),
        "norm_mlp_post": gain(ks[14], (L, D_MODEL)),
        "w_up": nrm(ks[15], (L, D_MODEL, 2 * D_FF), D_MODEL ** -0.5),
        "conv_w": nrm(ks[16], (L, CONV_WIDTH, 2 * D_FF), CONV_WIDTH ** -0.5),
        "conv_b": nrm(ks[17], (L, 2 * D_FF), 0.02),
        "w_down": nrm(ks[18], (L, D_FF, D_MODEL), D_FF ** -0.5),
    }


def reference(x, norm_mix_pre, norm_mix_post, w_in, lambda_q1, lambda_k1, lambda_q2, lambda_k2,
              diff_subln, ret_norm, pool_w, pool_scale, w_out, norm_mlp_pre, norm_mlp_post,
              w_up, conv_w, conv_b, w_down):
    split_points = []
    acc = 0
    for s in SPLIT_SIZES[:-1]:
        acc += s
        split_points.append(acc)
    for l in range(DEPTH):
        lam_init = 0.8 - 0.6 * math.exp(-0.3 * l)
        h = rms_norm(x, norm_mix_pre[l])
        proj = h @ w_in[l]
        q_da, k_da, v_da, q_r, k_r, v_r, g_r, u_pool = jnp.split(proj, split_points, axis=-1)
        o_da = diff_attention(q_da, k_da, v_da, lambda_q1[l], lambda_k1[l], lambda_q2[l],
                              lambda_k2[l], diff_subln[l], lam_init)
        o_ret = retention(q_r, k_r, v_r, g_r, ret_norm[l])
        o_pool = pool_mixer(u_pool, pool_w[l], pool_scale[l])
        mix = jnp.concatenate([o_da, o_ret, o_pool], axis=-1) @ w_out[l]
        x = x + rms_norm(mix, norm_mix_post[l])
        h = rms_norm(x, norm_mlp_pre[l])
        y = conv_mlp(h, w_up[l], conv_w[l], conv_b[l], w_down[l])
        x = x + rms_norm(y, norm_mlp_post[l])
    return x
```

```python
import functools
import math

import jax
import jax.numpy as jnp
from jax import lax
from jax.experimental import pallas as pl
from jax.experimental.pallas import tpu as pltpu

F32 = jnp.float32
BF16 = jnp.bfloat16

D_MODEL = 1024
DA_HEADS = 4
DA_QK_DIM = 64
DA_V_DIM = 128
DA_ROT_DIM = 16
ROPE_THETA = 500000.0
RET_HEADS = 4
RET_QK_DIM = 32
RET_V_DIM = 64
RET_THETA = 10000.0
RET_CHUNK = 128
POOL_GROUPS = 4
POOL_DIM = 64
POOL_WINDOWS = (2, 4, 8, 16)
POOL_HALO = 16
DA_WIDTH = 512
RET_WIDTH = 256
POOL_WIDTH = 256
IN_WIDTH = 2560
D_FF = 2816
CONV_WIDTH = 3
EPS = 1e-6

COL_Q_DA, COL_K_DA, COL_V_DA = 0, 512, 1024
COL_QK_R, COL_V_R, COL_G_R, COL_U = 1536, 1792, 2048, 2304

LANES = 128
SUBLANES = 8
MXU_COLS = 256
VMEM_LIMIT = 56 * 1024 * 1024

TM_PROJ = 512
TQ = 512
R_RET = 512


def _params(*sem):
    return pltpu.CompilerParams(dimension_semantics=sem, vmem_limit_bytes=VMEM_LIMIT)


def _rms(x, g):
    return x * lax.rsqrt(jnp.mean(x * x, axis=-1, keepdims=True) + EPS) * g


def _rot(x, tab_ref, shift):
    return (x * tab_ref[0]
            + pltpu.roll(x, shift, 1) * tab_ref[1]
            + pltpu.roll(x, LANES - shift, 1) * tab_ref[2])


def _in_proj_kernel(x_ref, g_ref, w_ref, tda_ref, tret_ref, o_ref):
    h = _rms(x_ref[...], g_ref[...]).astype(BF16)
    for j in range(IN_WIDTH // MXU_COLS):
        c0 = j * MXU_COLS
        p = jnp.dot(h, w_ref[:, c0:c0 + MXU_COLS], preferred_element_type=F32)
        lo, hi = p[:, :LANES], p[:, LANES:]
        if c0 < COL_V_DA:
            lo = _rot(lo, tda_ref, DA_ROT_DIM // 2)
            hi = _rot(hi, tda_ref, DA_ROT_DIM // 2)
            if c0 < COL_K_DA:
                lo = lo * (DA_QK_DIM ** -0.5)
                hi = hi * (DA_QK_DIM ** -0.5)
        elif c0 == COL_QK_R:
            lo = _rot(lo, tret_ref, RET_QK_DIM // 2)
            hi = _rot(hi, tret_ref, RET_QK_DIM // 2) * (RET_QK_DIM ** -0.5)
        o_ref[:, c0:c0 + LANES] = lo.astype(BF16)
        o_ref[:, c0 + LANES:c0 + MXU_COLS] = hi.astype(BF16)


def _in_proj(x, g, w, tda, tret, seq):
    n = x.shape[0]
    tiles_per_seq = seq // TM_PROJ
    return pl.pallas_call(
        _in_proj_kernel,
        grid=(n // TM_PROJ,),
        in_specs=[
            pl.BlockSpec((TM_PROJ, D_MODEL), lambda i: (i, 0)),
            pl.BlockSpec((1, D_MODEL), lambda i: (0, 0)),
            pl.BlockSpec((D_MODEL, IN_WIDTH), lambda i: (0, 0)),
            pl.BlockSpec((3, TM_PROJ, LANES), lambda i: (0, i % tiles_per_seq, 0)),
            pl.BlockSpec((3, TM_PROJ, LANES), lambda i: (0, i % tiles_per_seq, 0)),
        ],
        out_specs=pl.BlockSpec((TM_PROJ, IN_WIDTH), lambda i: (i, 0)),
        out_shape=jax.ShapeDtypeStruct((n, IN_WIDTH), BF16),
        compiler_params=_params("arbitrary"),
        name="in_proj",
    )(x, g, w, tda, tret)


def _diff_attn_kernel(q_ref, k_ref, v_ref, lq1_ref, lk1_ref, lq2_ref, lk2_ref, g_ref, o_ref,
                      vext_ref, m_ref, acc_ref, *, lam_init):
    qi = pl.program_id(2)

    @pl.when(qi == 0)
    def _():
        vext_ref[:, :LANES] = v_ref[...]
        vext_ref[:, LANES:] = jnp.ones((v_ref.shape[0], LANES), BF16)

    q = q_ref[...]
    lane = lax.broadcasted_iota(jnp.int32, q.shape, 1)
    zero = jnp.zeros_like(q)
    q_maps = (jnp.where(lane < DA_QK_DIM, q, zero), jnp.where(lane >= DA_QK_DIM, q, zero))

    m_ref[...] = jnp.full(m_ref.shape, -jnp.inf, F32)
    acc_ref[...] = jnp.zeros(acc_ref.shape, F32)

    def chunk(kv, masked):
        start = pl.multiple_of(kv * TQ, TQ)
        ks = k_ref[pl.ds(start, TQ), :]
        vs = vext_ref[pl.ds(start, TQ), :]
        for mi in range(2):
            s = lax.dot_general(q_maps[mi], ks, (((1,), (1,)), ((), ())), preferred_element_type=F32)
            if masked:
                row = lax.broadcasted_iota(jnp.int32, s.shape, 0)
                col = lax.broadcasted_iota(jnp.int32, s.shape, 1)
                s = jnp.where(col <= row, s, -jnp.inf)
            m_prev = m_ref[mi]
            m_new = jnp.maximum(m_prev, jnp.max(s, axis=1, keepdims=True))
            alpha = jnp.exp(m_prev - m_new)
            p = jnp.exp(s - pltpu.repeat(m_new, TQ // LANES, 1))
            acc_ref[mi] = (acc_ref[mi] * pltpu.repeat(alpha, 2, 1)
                           + jnp.dot(p.astype(BF16), vs, preferred_element_type=F32))
            m_ref[mi] = m_new

    def body(kv, carry):
        chunk(kv, False)
        return carry

    lax.fori_loop(0, qi, body, 0)
    chunk(qi, True)

    lam = (jnp.exp(jnp.sum(lq1_ref[...] * lk1_ref[...], axis=-1, keepdims=True))
           - jnp.exp(jnp.sum(lq2_ref[...] * lk2_ref[...], axis=-1, keepdims=True)) + lam_init)
    o1 = acc_ref[0, :, :LANES] / acc_ref[0, :, LANES:]
    o2 = acc_ref[1, :, :LANES] / acc_ref[1, :, LANES:]
    a = o1 - lam * o2
    o_ref[...] = (_rms(a, g_ref[...]) * (1.0 - lam_init)).astype(BF16)


def _diff_attn(proj, lq1, lk1, lq2, lk2, g, lam_init, batch, seq):
    n = proj.shape[0]
    nq = seq // TQ
    kcol, vcol = COL_K_DA // LANES, COL_V_DA // LANES
    vec = pl.BlockSpec((1, DA_QK_DIM), lambda b, h, i: (0, 0))
    return pl.pallas_call(
        functools.partial(_diff_attn_kernel, lam_init=lam_init),
        grid=(batch, DA_HEADS, nq),
        in_specs=[
            pl.BlockSpec((TQ, LANES), lambda b, h, i: (b * nq + i, h)),
            pl.BlockSpec((seq, LANES), lambda b, h, i: (b, kcol + h)),
            pl.BlockSpec((seq, LANES), lambda b, h, i: (b, vcol + h)),
            vec, vec, vec, vec,
            pl.BlockSpec((1, DA_V_DIM), lambda b, h, i: (0, 0)),
        ],
        out_specs=pl.BlockSpec((TQ, LANES), lambda b, h, i: (b * nq + i, h)),
        out_shape=jax.ShapeDtypeStruct((n, DA_WIDTH), BF16),
        scratch_shapes=[
            pltpu.VMEM((seq, 2 * LANES), BF16),
            pltpu.VMEM((2, TQ, LANES), F32),
            pltpu.VMEM((2, TQ, 2 * LANES), F32),
        ],
        compiler_params=_params("arbitrary", "arbitrary", "arbitrary"),
        name="diff_attn",
    )(proj, proj, proj, lq1, lk1, lq2, lk2, g)


def _split_dot(x, w):
    hi = x.astype(BF16)
    lo = (x - hi.astype(F32)).astype(BF16)
    return (jnp.dot(hi, w, preferred_element_type=F32) + jnp.dot(lo, w, preferred_element_type=F32))


def _ret_pool_kernel(qk_ref, v_ref, g_ref, u_ref, decay_ref, xi_ref, zeta_ref, cd_ref, bd_ref,
                     avg_ref, retg_ref, win_ref, pw_ref, ps_ref, o_ref,
                     state_ref, halo_ref, ext_ref, oret_ref):
    i = pl.program_id(1)

    @pl.when(i == 0)
    def _():
        state_ref[...] = jnp.zeros(state_ref.shape, F32)
        halo_ref[...] = jnp.zeros(halo_ref.shape, F32)

    C = RET_CHUNK
    qlane = lax.broadcasted_iota(jnp.int32, (C, LANES), 1) // RET_QK_DIM
    vlane = lax.broadcasted_iota(jnp.int32, (C, RET_WIDTH), 1) // RET_V_DIM
    for c in range(R_RET // C):
        r0 = c * C
        qc = qk_ref[r0:r0 + C, :LANES]
        kc = qk_ref[r0:r0 + C, LANES:]
        vc = v_ref[r0:r0 + C, :]
        st = state_ref[...]
        o_c = jnp.dot((qc.astype(F32) * xi_ref[...]).astype(BF16), st.astype(BF16),
                      preferred_element_type=F32)
        for h in range(RET_HEADS):
            qh = jnp.where(qlane == h, qc, jnp.zeros_like(qc))
            inner = lax.dot_general(qh, kc, (((1,), (1,)), ((), ())),
                                    preferred_element_type=F32) * decay_ref[h]
            oh = jnp.dot(inner.astype(BF16), vc, preferred_element_type=F32)
            o_c = o_c + jnp.where(vlane == h, oh, 0.0)
        kz = (kc.astype(F32) * zeta_ref[...]).astype(BF16)
        upd = lax.dot_general(kz, vc, (((0,), (0,)), ((), ())), preferred_element_type=F32)
        state_ref[...] = st * cd_ref[...] + bd_ref[...] * upd
        oret_ref[r0:r0 + C, :] = o_c

    o = oret_ref[...]
    mu = _split_dot(o, avg_ref[...])
    d = o - mu
    var = _split_dot(d * d, avg_ref[...])
    y = d * lax.rsqrt(var + EPS) * retg_ref[...]
    gate = g_ref[...].astype(F32)
    o_ref[:, :RET_WIDTH] = (gate * jax.nn.sigmoid(gate) * y).astype(BF16)

    u = u_ref[...].astype(F32)
    ext_ref[:POOL_HALO, :] = halo_ref[...]
    ext_ref[POOL_HALO:, :] = u
    win = win_ref[...]
    psum = u
    for k in range(1, max(POOL_WINDOWS)):
        psum = psum + jnp.where(win > k, ext_ref[POOL_HALO - k:POOL_HALO - k + R_RET, :], 0.0)
    t = (i * R_RET + lax.broadcasted_iota(jnp.int32, u.shape, 0)).astype(F32)
    pooled = psum / jnp.minimum(t + 1.0, win) - u
    halo_ref[...] = ext_ref[R_RET:, :]
    yp = jnp.dot(pooled.astype(BF16), pw_ref[...], preferred_element_type=F32) * ps_ref[...]
    o_ref[:, RET_WIDTH:] = yp.astype(BF16)


def _ret_pool(proj, tabs, ret_g, pool_w_bd, pool_scale, batch, seq):
    n = proj.shape[0]
    nr = seq // R_RET
    decay, xi, zeta, cd, bd, avg, win = tabs

    def col(c):
        return pl.BlockSpec((R_RET, MXU_COLS), lambda b, i: (b * nr + i, c // MXU_COLS))

    def const(shape):
        return pl.BlockSpec(shape, lambda b, i: (0,) * len(shape))

    return pl.pallas_call(
        _ret_pool_kernel,
        grid=(batch, nr),
        in_specs=[
            col(COL_QK_R), col(COL_V_R), col(COL_G_R), col(COL_U),
            const(decay.shape), const(xi.shape), const(zeta.shape), const(cd.shape), const(bd.shape),
            const(avg.shape), const(ret_g.shape), const(win.shape), const(pool_w_bd.shape),
            const(pool_scale.shape),
        ],
        out_specs=pl.BlockSpec((R_RET, RET_WIDTH + POOL_WIDTH), lambda b, i: (b * nr + i, 0)),
        out_shape=jax.ShapeDtypeStruct((n, RET_WIDTH + POOL_WIDTH), BF16),
        scratch_shapes=[
            pltpu.VMEM((LANES, RET_WIDTH), F32),
            pltpu.VMEM((POOL_HALO, POOL_WIDTH), F32),
            pltpu.VMEM((R_RET + POOL_HALO, POOL_WIDTH), F32),
            pltpu.VMEM((R_RET, RET_WIDTH), F32),
        ],
        compiler_params=_params("arbitrary", "arbitrary"),
        name="ret_pool",
    )(proj, proj, proj, proj, decay, xi, zeta, cd, bd, avg, ret_g, win, pool_w_bd, pool_scale)


def _out_proj_kernel(oda_ref, orp_ref, x_ref, w_ref, gpost_ref, gpre_ref, xo_ref, h_ref):
    mix = (jnp.dot(oda_ref[...], w_ref[:DA_WIDTH, :], preferred_element_type=F32)
           + jnp.dot(orp_ref[...], w_ref[DA_WIDTH:, :], preferred_element_type=F32))
    xn = x_ref[...] + _rms(mix, gpost_ref[...])
    xo_ref[...] = xn
    h_ref[...] = _rms(xn, gpre_ref[...]).astype(BF16)


def _out_proj(o_da, o_rp, x, w, g_post, g_pre):
    n = x.shape[0]
    row = lambda width: pl.BlockSpec((TM_PROJ, width), lambda i: (i, 0))
    const = lambda shape: pl.BlockSpec(shape, lambda i: (0, 0))
    return pl.pallas_call(
        _out_proj_kernel,
        grid=(n // TM_PROJ,),
        in_specs=[row(DA_WIDTH), row(RET_WIDTH + POOL_WIDTH), row(D_MODEL),
                  const(w.shape), const(g_post.shape), const(g_pre.shape)],
        out_specs=[row(D_MODEL), row(D_MODEL)],
        out_shape=[jax.ShapeDtypeStruct((n, D_MODEL), F32), jax.ShapeDtypeStruct((n, D_MODEL), BF16)],
        compiler_params=_params("arbitrary"),
        name="out_proj",
    )(o_da, o_rp, x, w, g_post, g_pre)


def _up_glu_kernel(h_ref, w_ref, cw_ref, cb_ref, a_ref, carry_ref, ext_ref, *, tiles_per_seq):
    i = pl.program_id(0)
    tm = h_ref.shape[0]

    @pl.when(i % tiles_per_seq == 0)
    def _():
        carry_ref[...] = jnp.zeros(carry_ref.shape, F32)

    h = h_ref[...]

    def conv_cols(slot, c0):
        u = jnp.dot(h, w_ref[:, c0:c0 + MXU_COLS], preferred_element_type=F32)
        ext_ref[slot, :SUBLANES, :] = carry_ref[:, c0:c0 + MXU_COLS]
        ext_ref[slot, SUBLANES:, :] = u
        carry_ref[:, c0:c0 + MXU_COLS] = u[tm - SUBLANES:, :]
        cw = cw_ref[:, c0:c0 + MXU_COLS]
        return (cb_ref[:, c0:c0 + MXU_COLS]
                + cw[0:1] * ext_ref[slot, SUBLANES - 2:SUBLANES - 2 + tm, :]
                + cw[1:2] * ext_ref[slot, SUBLANES - 1:SUBLANES - 1 + tm, :]
                + cw[2:3] * u)

    for c in range(D_FF // MXU_COLS):
        c0 = c * MXU_COLS
        gate = conv_cols(0, c0)
        val = conv_cols(1, D_FF + c0)
        a_ref[:, c0:c0 + MXU_COLS] = (jax.nn.gelu(gate, approximate=True) * val).astype(BF16)


def _up_glu(h, w, cw, cb, seq):
    n = h.shape[0]
    const = lambda shape: pl.BlockSpec(shape, lambda i: (0, 0))
    return pl.pallas_call(
        functools.partial(_up_glu_kernel, tiles_per_seq=seq // TM_PROJ),
        grid=(n // TM_PROJ,),
        in_specs=[pl.BlockSpec((TM_PROJ, D_MODEL), lambda i: (i, 0)),
                  const(w.shape), const(cw.shape), const(cb.shape)],
        out_specs=pl.BlockSpec((TM_PROJ, D_FF), lambda i: (i, 0)),
        out_shape=jax.ShapeDtypeStruct((n, D_FF), BF16),
        scratch_shapes=[
            pltpu.VMEM((SUBLANES, 2 * D_FF), F32),
            pltpu.VMEM((2, TM_PROJ + SUBLANES, MXU_COLS), F32),
        ],
        compiler_params=_params("arbitrary"),
        name="up_glu",
    )(h, w, cw, cb)


def _down_proj_kernel(a_ref, x_ref, w_ref, g_ref, xo_ref):
    y = jnp.dot(a_ref[...], w_ref[...], preferred_element_type=F32)
    xo_ref[...] = x_ref[...] + _rms(y, g_ref[...])


def _down_proj(a, x, w, g):
    n = x.shape[0]
    row = lambda width: pl.BlockSpec((TM_PROJ, width), lambda i: (i, 0))
    const = lambda shape: pl.BlockSpec(shape, lambda i: (0, 0))
    return pl.pallas_call(
        _down_proj_kernel,
        grid=(n // TM_PROJ,),
        in_specs=[row(D_FF), row(D_MODEL), const(w.shape), const(g.shape)],
        out_specs=row(D_MODEL),
        out_shape=jax.ShapeDtypeStruct((n, D_MODEL), F32),
        compiler_params=_params("arbitrary"),
        name="down_proj",
    )(a, x, w, g)


def _rotary_tables(seq, rot_dim, theta, period):
    inv = jnp.float32(theta) ** (-jnp.arange(0, rot_dim, 2, dtype=F32) / rot_dim)
    ang = jnp.arange(seq, dtype=F32)[:, None] * inv[None, :]
    cos, sin = jnp.cos(ang), jnp.sin(ang)
    half = rot_dim // 2
    zh = jnp.zeros((seq, half), F32)
    zp = jnp.zeros((seq, period - rot_dim), F32)
    tabs = (jnp.concatenate([cos, cos, jnp.ones_like(zp)], axis=1),
            jnp.concatenate([zh, sin, zp], axis=1),
            jnp.concatenate([-sin, zh, zp], axis=1))
    return jnp.stack([jnp.tile(t, (1, LANES // period)) for t in tabs])


def _retention_tables():
    H, C = RET_HEADS, RET_CHUNK
    log_g = jnp.log(1.0 - 2.0 ** (-5.0 - jnp.arange(H, dtype=F32)))
    idx = jnp.arange(C, dtype=F32)
    diff = idx[:, None] - idx[None, :]
    decay = jnp.where(diff >= 0, jnp.exp(jnp.maximum(diff, 0.0) * log_g[:, None, None]), 0.0)
    xi = jnp.exp((idx + 1.0) * log_g[:, None])
    zeta = jnp.exp((C - 1.0 - idx) * log_g[:, None])
    chunk_decay = jnp.exp(C * log_g)
    xi_t = jnp.repeat(xi.T, RET_QK_DIM, axis=1)
    zeta_t = jnp.repeat(zeta.T, RET_QK_DIM, axis=1)
    row_head = jnp.arange(LANES) // RET_QK_DIM
    col_head = jnp.arange(RET_WIDTH) // RET_V_DIM
    bd = (row_head[:, None] == col_head[None, :]).astype(F32)
    cd = bd * chunk_decay[row_head][:, None]
    avg = ((col_head[:, None] == col_head[None, :]).astype(F32) / RET_V_DIM).astype(BF16)
    win = jnp.repeat(jnp.asarray(POOL_WINDOWS, F32), POOL_DIM)[None, :]
    return decay, xi_t, zeta_t, cd, bd, avg, win


def _block_diag(w):
    G, P, _ = w.shape
    eye = jnp.eye(G, dtype=w.dtype)
    return (eye[:, None, :, None] * w[:, :, None, :]).reshape(G * P, G * P)


def kernel(x, norm_mix_pre, norm_mix_post, w_in, lambda_q1, lambda_k1, lambda_q2, lambda_k2, diff_subln,
           ret_norm, pool_w, pool_scale, w_out, norm_mlp_pre, norm_mlp_post, w_up, conv_w, conv_b, w_down):
    batch, seq, _ = x.shape
    depth = w_in.shape[0]
    assert seq % TQ == 0 and seq % TM_PROJ == 0 and seq % R_RET == 0
    xf = x.reshape(batch * seq, D_MODEL)

    tda = _rotary_tables(seq, DA_ROT_DIM, ROPE_THETA, DA_QK_DIM)
    tret = _rotary_tables(seq, RET_QK_DIM, RET_THETA, RET_QK_DIM)
    rtabs = _retention_tables()
    row = lambda a: a.reshape(1, -1)

    for l in range(depth):
        lam_init = 0.8 - 0.6 * math.exp(-0.3 * l)
        proj = _in_proj(xf, row(norm_mix_pre[l]), w_in[l].astype(BF16), tda, tret, seq)
        o_da = _diff_attn(proj, row(lambda_q1[l]), row(lambda_k1[l]), row(lambda_q2[l]), row(lambda_k2[l]),
                          row(diff_subln[l]), lam_init, batch, seq)
        o_rp = _ret_pool(proj, rtabs, row(ret_norm[l]), _block_diag(pool_w[l]).astype(BF16),
                         row(pool_scale[l]), batch, seq)
        xf, h = _out_proj(o_da, o_rp, xf, w_out[l].astype(BF16), row(norm_mix_post[l]), row(norm_mlp_pre[l]))
        a = _up_glu(h, w_up[l].astype(BF16), conv_w[l], row(conv_b[l]), seq)
        xf = _down_proj(a, xf, w_down[l].astype(BF16), row(norm_mlp_post[l]))
    return xf.reshape(batch, seq, D_MODEL)
```

```python
import functools
import math

import jax
import jax.numpy as jnp
from jax import lax
from jax.experimental import pallas as pl
from jax.experimental.pallas import tpu as pltpu

F32 = jnp.float32
BF16 = jnp.bfloat16

D_MODEL = 1024
DA_HEADS = 4
DA_QK_DIM = 64
DA_V_DIM = 128
DA_ROT_DIM = 16
ROPE_THETA = 500000.0
RET_HEADS = 4
RET_QK_DIM = 32
RET_V_DIM = 64
RET_THETA = 10000.0
RET_CHUNK = 128
POOL_GROUPS = 4
POOL_DIM = 64
POOL_WINDOWS = (2, 4, 8, 16)
POOL_HALO = 16
DA_WIDTH = 512
RET_WIDTH = 256
POOL_WIDTH = 256
IN_WIDTH = 2560
D_FF = 2816
CONV_WIDTH = 3
EPS = 1e-6

COL_Q_DA, COL_K_DA, COL_V_DA = 0, 512, 1024
COL_QK_R, COL_V_R, COL_G_R, COL_U = 1536, 1792, 2048, 2304

LANES = 128
SUBLANES = 8
MXU_COLS = 256
VMEM_LIMIT = 56 * 1024 * 1024

TM_PROJ = 512
TQ = 1024
SUB = 256
QK_AHEAD = 3
VT_ROWS = DA_V_DIM + 16
LOG2E = math.log2(math.e)
R_RET = 512


def _params(*sem):
    return pltpu.CompilerParams(dimension_semantics=sem, vmem_limit_bytes=VMEM_LIMIT)


def _rms(x, g):
    return x * lax.rsqrt(jnp.mean(x * x, axis=-1, keepdims=True) + EPS) * g


def _rot(x, tab_ref, shift):
    return (x * tab_ref[0]
            + pltpu.roll(x, shift, 1) * tab_ref[1]
            + pltpu.roll(x, LANES - shift, 1) * tab_ref[2])


def _in_proj_kernel(x_ref, g_ref, w_ref, tda_ref, tret_ref, o_ref):
    h = _rms(x_ref[...], g_ref[...]).astype(BF16)
    for j in range(IN_WIDTH // MXU_COLS):
        c0 = j * MXU_COLS
        p = jnp.dot(h, w_ref[:, c0:c0 + MXU_COLS], preferred_element_type=F32)
        lo, hi = p[:, :LANES], p[:, LANES:]
        if c0 < COL_V_DA:
            lo = _rot(lo, tda_ref, DA_ROT_DIM // 2)
            hi = _rot(hi, tda_ref, DA_ROT_DIM // 2)
            if c0 < COL_K_DA:
                lo = lo * (DA_QK_DIM ** -0.5 * LOG2E)
                hi = hi * (DA_QK_DIM ** -0.5 * LOG2E)
        elif c0 == COL_QK_R:
            lo = _rot(lo, tret_ref, RET_QK_DIM // 2)
            hi = _rot(hi, tret_ref, RET_QK_DIM // 2) * (RET_QK_DIM ** -0.5)
        o_ref[:, c0:c0 + LANES] = lo.astype(BF16)
        o_ref[:, c0 + LANES:c0 + MXU_COLS] = hi.astype(BF16)


def _in_proj(x, g, w, tda, tret, seq):
    n = x.shape[0]
    tiles_per_seq = seq // TM_PROJ
    return pl.pallas_call(
        _in_proj_kernel,
        grid=(n // TM_PROJ,),
        in_specs=[
            pl.BlockSpec((TM_PROJ, D_MODEL), lambda i: (i, 0)),
            pl.BlockSpec((1, D_MODEL), lambda i: (0, 0)),
            pl.BlockSpec((D_MODEL, IN_WIDTH), lambda i: (0, 0)),
            pl.BlockSpec((3, TM_PROJ, LANES), lambda i: (0, i % tiles_per_seq, 0)),
            pl.BlockSpec((3, TM_PROJ, LANES), lambda i: (0, i % tiles_per_seq, 0)),
        ],
        out_specs=pl.BlockSpec((TM_PROJ, IN_WIDTH), lambda i: (i, 0)),
        out_shape=jax.ShapeDtypeStruct((n, IN_WIDTH), BF16),
        compiler_params=_params("arbitrary"),
        name="in_proj",
    )(x, g, w, tda, tret)


def _diff_attn_kernel(q_ref, k_ref, v_ref, lq1_ref, lk1_ref, lq2_ref, lk2_ref, g_ref, o_ref,
                      vt_ref, m_ref, acc_ref, *, lam_init):
    qi = pl.program_id(2)
    nkv = v_ref.shape[0] // TQ

    @pl.when(qi == 0)
    def _():
        for c in range(nkv):
            vt_ref[c, :DA_V_DIM, :] = v_ref[c * TQ:(c + 1) * TQ, :].astype(F32).T.astype(BF16)
            vt_ref[c, DA_V_DIM:, :] = jnp.ones((VT_ROWS - DA_V_DIM, TQ), BF16)

    qt = q_ref[...].astype(F32).T
    feat = lax.broadcasted_iota(jnp.int32, qt.shape, 0)
    qt_maps = (jnp.where(feat < DA_QK_DIM, qt, 0.0).astype(BF16),
               jnp.where(feat >= DA_QK_DIM, qt, 0.0).astype(BF16))

    m_ref[...] = jnp.full(m_ref.shape, -jnp.inf, F32)
    acc_ref[...] = jnp.zeros(acc_ref.shape, F32)

    def chunk(kv, masked):
        start = pl.multiple_of(kv * TQ, TQ)
        ks = k_ref[pl.ds(start, TQ), :]
        vt = vt_ref[kv]
        steps = [(kb, qb, mi) for kb in range(TQ // SUB)
                 for qb in range(kb if masked else 0, TQ // SUB) for mi in range(2)]

        def scores(kb, qb, mi):
            st = jnp.dot(ks[kb * SUB:(kb + 1) * SUB], qt_maps[mi][:, qb * SUB:(qb + 1) * SUB],
                         preferred_element_type=F32)
            if masked and kb == qb:
                key = lax.broadcasted_iota(jnp.int32, st.shape, 0)
                qry = lax.broadcasted_iota(jnp.int32, st.shape, 1)
                st = jnp.where(key <= qry, st, -jnp.inf)
            return st

        pending = [scores(*s) for s in steps[:QK_AHEAD]]
        for j, (kb, qb, mi) in enumerate(steps):
            st = pending.pop(0)
            if j + QK_AHEAD < len(steps):
                pending.append(scores(*steps[j + QK_AHEAD]))
            qs = slice(qb * SUB, (qb + 1) * SUB)
            m_prev = m_ref[mi, :, qs]
            m_new = jnp.maximum(m_prev, jnp.max(st, axis=0, keepdims=True))
            alpha = jnp.exp2(m_prev - m_new)
            p = jnp.exp2(st - m_new).astype(BF16)
            acc_ref[mi, :, qs] = (acc_ref[mi, :, qs] * alpha
                                  + jnp.dot(vt[:, kb * SUB:(kb + 1) * SUB], p, preferred_element_type=F32))
            m_ref[mi, :, qs] = m_new

    def body(kv, carry):
        chunk(kv, False)
        return carry

    lax.fori_loop(0, qi, body, 0)
    chunk(qi, True)

    lam = (jnp.exp(jnp.sum(lq1_ref[...] * lk1_ref[...], axis=-1, keepdims=True))
           - jnp.exp(jnp.sum(lq2_ref[...] * lk2_ref[...], axis=-1, keepdims=True)) + lam_init)
    o1 = acc_ref[0, :DA_V_DIM, :] / acc_ref[0, DA_V_DIM:DA_V_DIM + 1, :]
    o2 = acc_ref[1, :DA_V_DIM, :] / acc_ref[1, DA_V_DIM:DA_V_DIM + 1, :]
    a = (o1 - lam * o2).T
    o_ref[...] = (_rms(a, g_ref[...]) * (1.0 - lam_init)).astype(BF16)


def _diff_attn(proj, lq1, lk1, lq2, lk2, g, lam_init, batch, seq):
    n = proj.shape[0]
    nq = seq // TQ
    kcol, vcol = COL_K_DA // LANES, COL_V_DA // LANES
    vec = pl.BlockSpec((1, DA_QK_DIM), lambda b, h, i: (0, 0))
    return pl.pallas_call(
        functools.partial(_diff_attn_kernel, lam_init=lam_init),
        grid=(batch, DA_HEADS, nq),
        in_specs=[
            pl.BlockSpec((TQ, LANES), lambda b, h, i: (b * nq + i, h)),
            pl.BlockSpec((seq, LANES), lambda b, h, i: (b, kcol + h)),
            pl.BlockSpec((seq, LANES), lambda b, h, i: (b, vcol + h)),
            vec, vec, vec, vec,
            pl.BlockSpec((1, DA_V_DIM), lambda b, h, i: (0, 0)),
        ],
        out_specs=pl.BlockSpec((TQ, LANES), lambda b, h, i: (b * nq + i, h)),
        out_shape=jax.ShapeDtypeStruct((n, DA_WIDTH), BF16),
        scratch_shapes=[
            pltpu.VMEM((nq, VT_ROWS, TQ), BF16),
            pltpu.VMEM((2, 1, TQ), F32),
            pltpu.VMEM((2, VT_ROWS, TQ), F32),
        ],
        compiler_params=_params("arbitrary", "arbitrary", "arbitrary"),
        name="diff_attn",
    )(proj, proj, proj, lq1, lk1, lq2, lk2, g)


def _split_dot(x, w):
    hi = x.astype(BF16)
    lo = (x - hi.astype(F32)).astype(BF16)
    return (jnp.dot(hi, w, preferred_element_type=F32) + jnp.dot(lo, w, preferred_element_type=F32))


def _ret_pool_kernel(qk_ref, v_ref, g_ref, u_ref, decay_ref, xi_ref, zeta_ref, cd_ref, bd_ref,
                     avg_ref, retg_ref, win_ref, pw_ref, ps_ref, o_ref,
                     state_ref, halo_ref, ext_ref, oret_ref):
    i = pl.program_id(1)

    @pl.when(i == 0)
    def _():
        state_ref[...] = jnp.zeros(state_ref.shape, F32)
        halo_ref[...] = jnp.zeros(halo_ref.shape, F32)

    C = RET_CHUNK
    qlane = lax.broadcasted_iota(jnp.int32, (C, LANES), 1) // RET_QK_DIM
    vlane = lax.broadcasted_iota(jnp.int32, (C, RET_WIDTH), 1) // RET_V_DIM
    for c in range(R_RET // C):
        r0 = c * C
        qc = qk_ref[r0:r0 + C, :LANES]
        kc = qk_ref[r0:r0 + C, LANES:]
        vc = v_ref[r0:r0 + C, :]
        st = state_ref[...]
        o_c = jnp.dot((qc.astype(F32) * xi_ref[...]).astype(BF16), st.astype(BF16),
                      preferred_element_type=F32)
        for h in range(RET_HEADS):
            qh = jnp.where(qlane == h, qc, jnp.zeros_like(qc))
            inner = lax.dot_general(qh, kc, (((1,), (1,)), ((), ())),
                                    preferred_element_type=F32) * decay_ref[h]
            oh = jnp.dot(inner.astype(BF16), vc, preferred_element_type=F32)
            o_c = o_c + jnp.where(vlane == h, oh, 0.0)
        kz = (kc.astype(F32) * zeta_ref[...]).astype(BF16)
        upd = lax.dot_general(kz, vc, (((0,), (0,)), ((), ())), preferred_element_type=F32)
        state_ref[...] = st * cd_ref[...] + bd_ref[...] * upd
        oret_ref[r0:r0 + C, :] = o_c

    o = oret_ref[...]
    mu = _split_dot(o, avg_ref[...])
    d = o - mu
    var = _split_dot(d * d, avg_ref[...])
    y = d * lax.rsqrt(var + EPS) * retg_ref[...]
    gate = g_ref[...].astype(F32)
    o_ref[:, :RET_WIDTH] = (gate * jax.nn.sigmoid(gate) * y).astype(BF16)

    u = u_ref[...].astype(F32)
    ext_ref[:POOL_HALO, :] = halo_ref[...]
    ext_ref[POOL_HALO:, :] = u
    win = win_ref[...]
    psum = u
    for k in range(1, max(POOL_WINDOWS)):
        psum = psum + jnp.where(win > k, ext_ref[POOL_HALO - k:POOL_HALO - k + R_RET, :], 0.0)
    t = (i * R_RET + lax.broadcasted_iota(jnp.int32, u.shape, 0)).astype(F32)
    pooled = psum / jnp.minimum(t + 1.0, win) - u
    halo_ref[...] = ext_ref[R_RET:, :]
    yp = jnp.dot(pooled.astype(BF16), pw_ref[...], preferred_element_type=F32) * ps_ref[...]
    o_ref[:, RET_WIDTH:] = yp.astype(BF16)


def _ret_pool(proj, tabs, ret_g, pool_w_bd, pool_scale, batch, seq):
    n = proj.shape[0]
    nr = seq // R_RET
    decay, xi, zeta, cd, bd, avg, win = tabs

    def col(c):
        return pl.BlockSpec((R_RET, MXU_COLS), lambda b, i: (b * nr + i, c // MXU_COLS))

    def const(shape):
        return pl.BlockSpec(shape, lambda b, i: (0,) * len(shape))

    return pl.pallas_call(
        _ret_pool_kernel,
        grid=(batch, nr),
        in_specs=[
            col(COL_QK_R), col(COL_V_R), col(COL_G_R), col(COL_U),
            const(decay.shape), const(xi.shape), const(zeta.shape), const(cd.shape), const(bd.shape),
            const(avg.shape), const(ret_g.shape), const(win.shape), const(pool_w_bd.shape),
            const(pool_scale.shape),
        ],
        out_specs=pl.BlockSpec((R_RET, RET_WIDTH + POOL_WIDTH), lambda b, i: (b * nr + i, 0)),
        out_shape=jax.ShapeDtypeStruct((n, RET_WIDTH + POOL_WIDTH), BF16),
        scratch_shapes=[
            pltpu.VMEM((LANES, RET_WIDTH), F32),
            pltpu.VMEM((POOL_HALO, POOL_WIDTH), F32),
            pltpu.VMEM((R_RET + POOL_HALO, POOL_WIDTH), F32),
            pltpu.VMEM((R_RET, RET_WIDTH), F32),
        ],
        compiler_params=_params("arbitrary", "arbitrary"),
        name="ret_pool",
    )(proj, proj, proj, proj, decay, xi, zeta, cd, bd, avg, ret_g, win, pool_w_bd, pool_scale)


def _out_proj_kernel(oda_ref, orp_ref, x_ref, w_ref, gpost_ref, gpre_ref, xo_ref, h_ref):
    mix = (jnp.dot(oda_ref[...], w_ref[:DA_WIDTH, :], preferred_element_type=F32)
           + jnp.dot(orp_ref[...], w_ref[DA_WIDTH:, :], preferred_element_type=F32))
    xn = x_ref[...] + _rms(mix, gpost_ref[...])
    xo_ref[...] = xn
    h_ref[...] = _rms(xn, gpre_ref[...]).astype(BF16)


def _out_proj(o_da, o_rp, x, w, g_post, g_pre):
    n = x.shape[0]
    row = lambda width: pl.BlockSpec((TM_PROJ, width), lambda i: (i, 0))
    const = lambda shape: pl.BlockSpec(shape, lambda i: (0, 0))
    return pl.pallas_call(
        _out_proj_kernel,
        grid=(n // TM_PROJ,),
        in_specs=[row(DA_WIDTH), row(RET_WIDTH + POOL_WIDTH), row(D_MODEL),
                  const(w.shape), const(g_post.shape), const(g_pre.shape)],
        out_specs=[row(D_MODEL), row(D_MODEL)],
        out_shape=[jax.ShapeDtypeStruct((n, D_MODEL), F32), jax.ShapeDtypeStruct((n, D_MODEL), BF16)],
        compiler_params=_params("arbitrary"),
        name="out_proj",
    )(o_da, o_rp, x, w, g_post, g_pre)


def _up_glu_kernel(h_ref, w_ref, cw_ref, cb_ref, a_ref, carry_ref, ext_ref, *, tiles_per_seq):
    i = pl.program_id(0)
    tm = h_ref.shape[0]

    @pl.when(i % tiles_per_seq == 0)
    def _():
        carry_ref[...] = jnp.zeros(carry_ref.shape, F32)

    h = h_ref[...]

    def conv_cols(slot, c0):
        u = jnp.dot(h, w_ref[:, c0:c0 + MXU_COLS], preferred_element_type=F32)
        ext_ref[slot, :SUBLANES, :] = carry_ref[:, c0:c0 + MXU_COLS]
        ext_ref[slot, SUBLANES:, :] = u
        carry_ref[:, c0:c0 + MXU_COLS] = u[tm - SUBLANES:, :]
        cw = cw_ref[:, c0:c0 + MXU_COLS]
        return (cb_ref[:, c0:c0 + MXU_COLS]
                + cw[0:1] * ext_ref[slot, SUBLANES - 2:SUBLANES - 2 + tm, :]
                + cw[1:2] * ext_ref[slot, SUBLANES - 1:SUBLANES - 1 + tm, :]
                + cw[2:3] * u)

    for c in range(D_FF // MXU_COLS):
        c0 = c * MXU_COLS
        gate = conv_cols(0, c0)
        val = conv_cols(1, D_FF + c0)
        a_ref[:, c0:c0 + MXU_COLS] = (jax.nn.gelu(gate, approximate=True) * val).astype(BF16)


def _up_glu(h, w, cw, cb, seq):
    n = h.shape[0]
    const = lambda shape: pl.BlockSpec(shape, lambda i: (0, 0))
    return pl.pallas_call(
        functools.partial(_up_glu_kernel, tiles_per_seq=seq // TM_PROJ),
        grid=(n // TM_PROJ,),
        in_specs=[pl.BlockSpec((TM_PROJ, D_MODEL), lambda i: (i, 0)),
                  const(w.shape), const(cw.shape), const(cb.shape)],
        out_specs=pl.BlockSpec((TM_PROJ, D_FF), lambda i: (i, 0)),
        out_shape=jax.ShapeDtypeStruct((n, D_FF), BF16),
        scratch_shapes=[
            pltpu.VMEM((SUBLANES, 2 * D_FF), F32),
            pltpu.VMEM((2, TM_PROJ + SUBLANES, MXU_COLS), F32),
        ],
        compiler_params=_params("arbitrary"),
        name="up_glu",
    )(h, w, cw, cb)


def _down_proj_kernel(a_ref, x_ref, w_ref, g_ref, xo_ref):
    y = jnp.dot(a_ref[...], w_ref[...], preferred_element_type=F32)
    xo_ref[...] = x_ref[...] + _rms(y, g_ref[...])


def _down_proj(a, x, w, g):
    n = x.shape[0]
    row = lambda width: pl.BlockSpec((TM_PROJ, width), lambda i: (i, 0))
    const = lambda shape: pl.BlockSpec(shape, lambda i: (0, 0))
    return pl.pallas_call(
        _down_proj_kernel,
        grid=(n // TM_PROJ,),
        in_specs=[row(D_FF), row(D_MODEL), const(w.shape), const(g.shape)],
        out_specs=row(D_MODEL),
        out_shape=jax.ShapeDtypeStruct((n, D_MODEL), F32),
        compiler_params=_params("arbitrary"),
        name="down_proj",
    )(a, x, w, g)


def _rotary_tables(seq, rot_dim, theta, period):
    inv = jnp.float32(theta) ** (-jnp.arange(0, rot_dim, 2, dtype=F32) / rot_dim)
    ang = jnp.arange(seq, dtype=F32)[:, None] * inv[None, :]
    cos, sin = jnp.cos(ang), jnp.sin(ang)
    half = rot_dim // 2
    zh = jnp.zeros((seq, half), F32)
    zp = jnp.zeros((seq, period - rot_dim), F32)
    tabs = (jnp.concatenate([cos, cos, jnp.ones_like(zp)], axis=1),
            jnp.concatenate([zh, sin, zp], axis=1),
            jnp.concatenate([-sin, zh, zp], axis=1))
    return jnp.stack([jnp.tile(t, (1, LANES // period)) for t in tabs])


def _retention_tables():
    H, C = RET_HEADS, RET_CHUNK
    log_g = jnp.log(1.0 - 2.0 ** (-5.0 - jnp.arange(H, dtype=F32)))
    idx = jnp.arange(C, dtype=F32)
    diff = idx[:, None] - idx[None, :]
    decay = jnp.where(diff >= 0, jnp.exp(jnp.maximum(diff, 0.0) * log_g[:, None, None]), 0.0)
    xi = jnp.exp((idx + 1.0) * log_g[:, None])
    zeta = jnp.exp((C - 1.0 - idx) * log_g[:, None])
    chunk_decay = jnp.exp(C * log_g)
    xi_t = jnp.repeat(xi.T, RET_QK_DIM, axis=1)
    zeta_t = jnp.repeat(zeta.T, RET_QK_DIM, axis=1)
    row_head = jnp.arange(LANES) // RET_QK_DIM
    col_head = jnp.arange(RET_WIDTH) // RET_V_DIM
    bd = (row_head[:, None] == col_head[None, :]).astype(F32)
    cd = bd * chunk_decay[row_head][:, None]
    avg = ((col_head[:, None] == col_head[None, :]).astype(F32) / RET_V_DIM).astype(BF16)
    win = jnp.repeat(jnp.asarray(POOL_WINDOWS, F32), POOL_DIM)[None, :]
    return decay, xi_t, zeta_t, cd, bd, avg, win


def _block_diag(w):
    G, P, _ = w.shape
    eye = jnp.eye(G, dtype=w.dtype)
    return (eye[:, None, :, None] * w[:, :, None, :]).reshape(G * P, G * P)


def kernel(x, norm_mix_pre, norm_mix_post, w_in, lambda_q1, lambda_k1, lambda_q2, lambda_k2, diff_subln,
           ret_norm, pool_w, pool_scale, w_out, norm_mlp_pre, norm_mlp_post, w_up, conv_w, conv_b, w_down):
    batch, seq, _ = x.shape
    depth = w_in.shape[0]
    assert seq % TQ == 0 and seq % TM_PROJ == 0 and seq % R_RET == 0
    xf = x.reshape(batch * seq, D_MODEL)

    tda = _rotary_tables(seq, DA_ROT_DIM, ROPE_THETA, DA_QK_DIM)
    tret = _rotary_tables(seq, RET_QK_DIM, RET_THETA, RET_QK_DIM)
    rtabs = _retention_tables()
    row = lambda a: a.reshape(1, -1)

    for l in range(depth):
        lam_init = 0.8 - 0.6 * math.exp(-0.3 * l)
        proj = _in_proj(xf, row(norm_mix_pre[l]), w_in[l].astype(BF16), tda, tret, seq)
        o_da = _diff_attn(proj, row(lambda_q1[l]), row(lambda_k1[l]), row(lambda_q2[l]), row(lambda_k2[l]),
                          row(diff_subln[l]), lam_init, batch, seq)
        o_rp = _ret_pool(proj, rtabs, row(ret_norm[l]), _block_diag(pool_w[l]).astype(BF16),
                         row(pool_scale[l]), batch, seq)
        xf, h = _out_proj(o_da, o_rp, xf, w_out[l].astype(BF16), row(norm_mix_post[l]), row(norm_mlp_pre[l]))
        a = _up_glu(h, w_up[l].astype(BF16), conv_w[l], row(conv_b[l]), seq)
        xf = _down_proj(a, xf, w_down[l].astype(BF16), row(norm_mlp_post[l]))
    return xf.reshape(batch, seq, D_MODEL)
```

```python
import functools
import math

import jax
import jax.numpy as jnp
from jax import lax
from jax.experimental import pallas as pl
from jax.experimental.pallas import tpu as pltpu

F32 = jnp.float32
BF16 = jnp.bfloat16

D_MODEL = 1024
DA_HEADS = 4
DA_QK_DIM = 64
DA_V_DIM = 128
DA_ROT_DIM = 16
ROPE_THETA = 500000.0
RET_HEADS = 4
RET_QK_DIM = 32
RET_V_DIM = 64
RET_THETA = 10000.0
RET_CHUNK = 128
POOL_GROUPS = 4
POOL_DIM = 64
POOL_WINDOWS = (2, 4, 8, 16)
POOL_HALO = 16
DA_WIDTH = 512
RET_WIDTH = 256
POOL_WIDTH = 256
IN_WIDTH = 2560
D_FF = 2816
CONV_WIDTH = 3
EPS = 1e-6

COL_Q_DA, COL_K_DA, COL_V_DA = 0, 512, 1024
COL_QK_R, COL_V_R, COL_G_R, COL_U = 1536, 1792, 2048, 2304

LANES = 128
SUBLANES = 8
MXU_COLS = 256
VMEM_LIMIT = 56 * 1024 * 1024

TM_PROJ = 512
TQ = 1024
TK = 512
SUB = 256
QK_AHEAD = 2
VT_ROWS = DA_V_DIM + 16
LOG2E = math.log2(math.e)
R_RET = 512


def _params(*sem):
    return pltpu.CompilerParams(dimension_semantics=sem, vmem_limit_bytes=VMEM_LIMIT)


def _rms(x, g):
    return x * lax.rsqrt(jnp.mean(x * x, axis=-1, keepdims=True) + EPS) * g


def _rot(x, tab_ref, shift):
    return (x * tab_ref[0]
            + pltpu.roll(x, shift, 1) * tab_ref[1]
            + pltpu.roll(x, LANES - shift, 1) * tab_ref[2])


def _in_proj_kernel(x_ref, g_ref, w_ref, tda_ref, tret_ref, o_ref):
    h = _rms(x_ref[...], g_ref[...]).astype(BF16)
    for j in range(IN_WIDTH // MXU_COLS):
        c0 = j * MXU_COLS
        p = jnp.dot(h, w_ref[:, c0:c0 + MXU_COLS], preferred_element_type=F32)
        lo, hi = p[:, :LANES], p[:, LANES:]
        if c0 < COL_V_DA:
            lo = _rot(lo, tda_ref, DA_ROT_DIM // 2)
            hi = _rot(hi, tda_ref, DA_ROT_DIM // 2)
            if c0 < COL_K_DA:
                lo = lo * (DA_QK_DIM ** -0.5 * LOG2E)
                hi = hi * (DA_QK_DIM ** -0.5 * LOG2E)
        elif c0 == COL_QK_R:
            lo = _rot(lo, tret_ref, RET_QK_DIM // 2)
            hi = _rot(hi, tret_ref, RET_QK_DIM // 2) * (RET_QK_DIM ** -0.5)
        o_ref[:, c0:c0 + LANES] = lo.astype(BF16)
        o_ref[:, c0 + LANES:c0 + MXU_COLS] = hi.astype(BF16)


def _in_proj(x, g, w, tda, tret, seq):
    n = x.shape[0]
    tiles_per_seq = seq // TM_PROJ
    return pl.pallas_call(
        _in_proj_kernel,
        grid=(n // TM_PROJ,),
        in_specs=[
            pl.BlockSpec((TM_PROJ, D_MODEL), lambda i: (i, 0)),
            pl.BlockSpec((1, D_MODEL), lambda i: (0, 0)),
            pl.BlockSpec((D_MODEL, IN_WIDTH), lambda i: (0, 0)),
            pl.BlockSpec((3, TM_PROJ, LANES), lambda i: (0, i % tiles_per_seq, 0)),
            pl.BlockSpec((3, TM_PROJ, LANES), lambda i: (0, i % tiles_per_seq, 0)),
        ],
        out_specs=pl.BlockSpec((TM_PROJ, IN_WIDTH), lambda i: (i, 0)),
        out_shape=jax.ShapeDtypeStruct((n, IN_WIDTH), BF16),
        compiler_params=_params("arbitrary"),
        name="in_proj",
    )(x, g, w, tda, tret)


def _diff_attn_kernel(q_ref, k_ref, v_ref, lq1_ref, lk1_ref, lq2_ref, lk2_ref, g_ref, o_ref,
                      vt_ref, qt_ref, s0_ref, s1_ref, mx0_ref, mx1_ref, m_ref, acc_ref, *, lam_init):
    qi = pl.program_id(2)
    nkv = v_ref.shape[0] // TK
    s_refs, mx_refs = (s0_ref, s1_ref), (mx0_ref, mx1_ref)

    @pl.when(qi == 0)
    def _():
        for c in range(nkv):
            vt_ref[c, :DA_V_DIM, :] = v_ref[c * TK:(c + 1) * TK, :].astype(F32).T.astype(BF16)
            vt_ref[c, DA_V_DIM:, :] = jnp.ones((VT_ROWS - DA_V_DIM, TK), BF16)

    qt = q_ref[...].astype(F32).T
    feat = lax.broadcasted_iota(jnp.int32, qt.shape, 0)
    qt_ref[0] = jnp.where(feat < DA_QK_DIM, qt, 0.0).astype(BF16)
    qt_ref[1] = jnp.where(feat >= DA_QK_DIM, qt, 0.0).astype(BF16)

    m_ref[...] = jnp.full(m_ref.shape, -jnp.inf, F32)
    acc_ref[...] = jnp.zeros(acc_ref.shape, F32)

    def columns(diag):
        out = []
        for qb in range(TQ // SUB):
            if diag is not None and diag * TK > (qb + 1) * SUB - 1:
                continue
            mask = diag is not None and (diag + 1) * TK - 1 > qb * SUB
            out += [(2 * qb + mi, qb, mi, mask) for mi in range(2)]
        return out

    def score(c, slot, diag):
        def one(j, qb, mi, mask):
            ks = k_ref[pl.ds(pl.multiple_of(c * TK, TK), TK), :]
            st = jnp.dot(ks, qt_ref[mi, :, qb * SUB:(qb + 1) * SUB],
                         preferred_element_type=F32)
            if mask:
                key = diag * TK + lax.broadcasted_iota(jnp.int32, st.shape, 0)
                qry = qb * SUB + lax.broadcasted_iota(jnp.int32, st.shape, 1)
                st = jnp.where(key <= qry, st, -jnp.inf)
            s_refs[slot][j] = st
            mx_refs[slot][j] = jnp.max(st, axis=0, keepdims=True)
        return [functools.partial(one, *col) for col in columns(diag)]

    def update(c, slot, diag):
        def one(j, qb, mi, mask):
            qs = slice(qb * SUB, (qb + 1) * SUB)
            m_prev = m_ref[mi, :, qs]
            m_new = jnp.maximum(m_prev, mx_refs[slot][j])
            alpha = jnp.exp2(m_prev - m_new)
            p = jnp.exp2(s_refs[slot][j] - m_new).astype(BF16)
            acc_ref[mi, :, qs] = (acc_ref[mi, :, qs] * alpha
                                  + jnp.dot(vt_ref[c], p, preferred_element_type=F32))
            m_ref[mi, :, qs] = m_new
        return [functools.partial(one, *col) for col in columns(diag)]

    def emit(scores, updates):
        scores = list(scores)
        for s in scores[:QK_AHEAD]:
            s()
        rest = scores[QK_AHEAD:]
        for u in updates:
            u()
            if rest:
                rest.pop(0)()
        for s in rest:
            s()

    assert TQ // TK == 2
    c0 = 2 * qi

    @pl.when(qi > 0)
    def _():
        emit(score(0, 0, None), [])

        def body(i, carry):
            emit(score(2 * i + 1, 1, None), update(2 * i, 0, None))
            emit(score(2 * i + 2, 0, None), update(2 * i + 1, 1, None))
            return carry

        lax.fori_loop(0, qi - 1, body, 0)
        emit(score(c0 - 1, 1, None), update(c0 - 2, 0, None))
        emit(score(c0, 0, 0), update(c0 - 1, 1, None))

    @pl.when(qi == 0)
    def _():
        emit(score(c0, 0, 0), [])

    emit(score(c0 + 1, 1, 1), update(c0, 0, 0))
    emit([], update(c0 + 1, 1, 1))

    lam = (jnp.exp(jnp.sum(lq1_ref[...] * lk1_ref[...], axis=-1, keepdims=True))
           - jnp.exp(jnp.sum(lq2_ref[...] * lk2_ref[...], axis=-1, keepdims=True)) + lam_init)
    o1 = acc_ref[0, :DA_V_DIM, :] / acc_ref[0, DA_V_DIM:DA_V_DIM + 1, :]
    o2 = acc_ref[1, :DA_V_DIM, :] / acc_ref[1, DA_V_DIM:DA_V_DIM + 1, :]
    a = (o1 - lam * o2).T
    o_ref[...] = (_rms(a, g_ref[...]) * (1.0 - lam_init)).astype(BF16)


def _diff_attn(proj, lq1, lk1, lq2, lk2, g, lam_init, batch, seq):
    n = proj.shape[0]
    nq = seq // TQ
    kcol, vcol = COL_K_DA // LANES, COL_V_DA // LANES
    vec = pl.BlockSpec((1, DA_QK_DIM), lambda b, h, i: (0, 0))
    return pl.pallas_call(
        functools.partial(_diff_attn_kernel, lam_init=lam_init),
        grid=(batch, DA_HEADS, nq),
        in_specs=[
            pl.BlockSpec((TQ, LANES), lambda b, h, i: (b * nq + i, h)),
            pl.BlockSpec((seq, LANES), lambda b, h, i: (b, kcol + h)),
            pl.BlockSpec((seq, LANES), lambda b, h, i: (b, vcol + h)),
            vec, vec, vec, vec,
            pl.BlockSpec((1, DA_V_DIM), lambda b, h, i: (0, 0)),
        ],
        out_specs=pl.BlockSpec((TQ, LANES), lambda b, h, i: (b * nq + i, h)),
        out_shape=jax.ShapeDtypeStruct((n, DA_WIDTH), BF16),
        scratch_shapes=[
            pltpu.VMEM((seq // TK, VT_ROWS, TK), BF16),
            pltpu.VMEM((2, LANES, TQ), BF16),
            pltpu.VMEM((2 * TQ // SUB, TK, SUB), F32),
            pltpu.VMEM((2 * TQ // SUB, TK, SUB), F32),
            pltpu.VMEM((2 * TQ // SUB, 1, SUB), F32),
            pltpu.VMEM((2 * TQ // SUB, 1, SUB), F32),
            pltpu.VMEM((2, 1, TQ), F32),
            pltpu.VMEM((2, VT_ROWS, TQ), F32),
        ],
        compiler_params=_params("arbitrary", "arbitrary", "arbitrary"),
        name="diff_attn",
    )(proj, proj, proj, lq1, lk1, lq2, lk2, g)


def _split_dot(x, w):
    hi = x.astype(BF16)
    lo = (x - hi.astype(F32)).astype(BF16)
    return (jnp.dot(hi, w, preferred_element_type=F32) + jnp.dot(lo, w, preferred_element_type=F32))


def _ret_pool_kernel(qk_ref, v_ref, g_ref, u_ref, decay_ref, xi_ref, zeta_ref, cd_ref, bd_ref,
                     avg_ref, retg_ref, win_ref, pw_ref, ps_ref, o_ref,
                     state_ref, halo_ref, ext_ref, oret_ref):
    i = pl.program_id(1)

    @pl.when(i == 0)
    def _():
        state_ref[...] = jnp.zeros(state_ref.shape, F32)
        halo_ref[...] = jnp.zeros(halo_ref.shape, F32)

    C = RET_CHUNK
    qlane = lax.broadcasted_iota(jnp.int32, (C, LANES), 1) // RET_QK_DIM
    vlane = lax.broadcasted_iota(jnp.int32, (C, RET_WIDTH), 1) // RET_V_DIM
    for c in range(R_RET // C):
        r0 = c * C
        qc = qk_ref[r0:r0 + C, :LANES]
        kc = qk_ref[r0:r0 + C, LANES:]
        vc = v_ref[r0:r0 + C, :]
        st = state_ref[...]
        o_c = jnp.dot((qc.astype(F32) * xi_ref[...]).astype(BF16), st.astype(BF16),
                      preferred_element_type=F32)
        for h in range(RET_HEADS):
            qh = jnp.where(qlane == h, qc, jnp.zeros_like(qc))
            inner = lax.dot_general(qh, kc, (((1,), (1,)), ((), ())),
                                    preferred_element_type=F32) * decay_ref[h]
            oh = jnp.dot(inner.astype(BF16), vc, preferred_element_type=F32)
            o_c = o_c + jnp.where(vlane == h, oh, 0.0)
        kz = (kc.astype(F32) * zeta_ref[...]).astype(BF16)
        upd = lax.dot_general(kz, vc, (((0,), (0,)), ((), ())), preferred_element_type=F32)
        state_ref[...] = st * cd_ref[...] + bd_ref[...] * upd
        oret_ref[r0:r0 + C, :] = o_c

    o = oret_ref[...]
    mu = _split_dot(o, avg_ref[...])
    d = o - mu
    var = _split_dot(d * d, avg_ref[...])
    y = d * lax.rsqrt(var + EPS) * retg_ref[...]
    gate = g_ref[...].astype(F32)
    o_ref[:, :RET_WIDTH] = (gate * jax.nn.sigmoid(gate) * y).astype(BF16)

    u = u_ref[...].astype(F32)
    ext_ref[:POOL_HALO, :] = halo_ref[...]
    ext_ref[POOL_HALO:, :] = u
    win = win_ref[...]
    psum = u
    for k in range(1, max(POOL_WINDOWS)):
        psum = psum + jnp.where(win > k, ext_ref[POOL_HALO - k:POOL_HALO - k + R_RET, :], 0.0)
    t = (i * R_RET + lax.broadcasted_iota(jnp.int32, u.shape, 0)).astype(F32)
    pooled = psum / jnp.minimum(t + 1.0, win) - u
    halo_ref[...] = ext_ref[R_RET:, :]
    yp = jnp.dot(pooled.astype(BF16), pw_ref[...], preferred_element_type=F32) * ps_ref[...]
    o_ref[:, RET_WIDTH:] = yp.astype(BF16)


def _ret_pool(proj, tabs, ret_g, pool_w_bd, pool_scale, batch, seq):
    n = proj.shape[0]
    nr = seq // R_RET
    decay, xi, zeta, cd, bd, avg, win = tabs

    def col(c):
        return pl.BlockSpec((R_RET, MXU_COLS), lambda b, i: (b * nr + i, c // MXU_COLS))

    def const(shape):
        return pl.BlockSpec(shape, lambda b, i: (0,) * len(shape))

    return pl.pallas_call(
        _ret_pool_kernel,
        grid=(batch, nr),
        in_specs=[
            col(COL_QK_R), col(COL_V_R), col(COL_G_R), col(COL_U),
            const(decay.shape), const(xi.shape), const(zeta.shape), const(cd.shape), const(bd.shape),
            const(avg.shape), const(ret_g.shape), const(win.shape), const(pool_w_bd.shape),
            const(pool_scale.shape),
        ],
        out_specs=pl.BlockSpec((R_RET, RET_WIDTH + POOL_WIDTH), lambda b, i: (b * nr + i, 0)),
        out_shape=jax.ShapeDtypeStruct((n, RET_WIDTH + POOL_WIDTH), BF16),
        scratch_shapes=[
            pltpu.VMEM((LANES, RET_WIDTH), F32),
            pltpu.VMEM((POOL_HALO, POOL_WIDTH), F32),
            pltpu.VMEM((R_RET + POOL_HALO, POOL_WIDTH), F32),
            pltpu.VMEM((R_RET, RET_WIDTH), F32),
        ],
        compiler_params=_params("arbitrary", "arbitrary"),
        name="ret_pool",
    )(proj, proj, proj, proj, decay, xi, zeta, cd, bd, avg, ret_g, win, pool_w_bd, pool_scale)


def _out_proj_kernel(oda_ref, orp_ref, x_ref, w_ref, gpost_ref, gpre_ref, xo_ref, h_ref):
    mix = (jnp.dot(oda_ref[...], w_ref[:DA_WIDTH, :], preferred_element_type=F32)
           + jnp.dot(orp_ref[...], w_ref[DA_WIDTH:, :], preferred_element_type=F32))
    xn = x_ref[...] + _rms(mix, gpost_ref[...])
    xo_ref[...] = xn
    h_ref[...] = _rms(xn, gpre_ref[...]).astype(BF16)


def _out_proj(o_da, o_rp, x, w, g_post, g_pre):
    n = x.shape[0]
    row = lambda width: pl.BlockSpec((TM_PROJ, width), lambda i: (i, 0))
    const = lambda shape: pl.BlockSpec(shape, lambda i: (0, 0))
    return pl.pallas_call(
        _out_proj_kernel,
        grid=(n // TM_PROJ,),
        in_specs=[row(DA_WIDTH), row(RET_WIDTH + POOL_WIDTH), row(D_MODEL),
                  const(w.shape), const(g_post.shape), const(g_pre.shape)],
        out_specs=[row(D_MODEL), row(D_MODEL)],
        out_shape=[jax.ShapeDtypeStruct((n, D_MODEL), F32), jax.ShapeDtypeStruct((n, D_MODEL), BF16)],
        compiler_params=_params("arbitrary"),
        name="out_proj",
    )(o_da, o_rp, x, w, g_post, g_pre)


def _up_glu_kernel(h_ref, w_ref, cw_ref, cb_ref, a_ref, carry_ref, ext_ref, *, tiles_per_seq):
    i = pl.program_id(0)
    tm = h_ref.shape[0]

    @pl.when(i % tiles_per_seq == 0)
    def _():
        carry_ref[...] = jnp.zeros(carry_ref.shape, F32)

    h = h_ref[...]

    def conv_cols(slot, c0):
        u = jnp.dot(h, w_ref[:, c0:c0 + MXU_COLS], preferred_element_type=F32)
        ext_ref[slot, :SUBLANES, :] = carry_ref[:, c0:c0 + MXU_COLS]
        ext_ref[slot, SUBLANES:, :] = u
        carry_ref[:, c0:c0 + MXU_COLS] = u[tm - SUBLANES:, :]
        cw = cw_ref[:, c0:c0 + MXU_COLS]
        return (cb_ref[:, c0:c0 + MXU_COLS]
                + cw[0:1] * ext_ref[slot, SUBLANES - 2:SUBLANES - 2 + tm, :]
                + cw[1:2] * ext_ref[slot, SUBLANES - 1:SUBLANES - 1 + tm, :]
                + cw[2:3] * u)

    for c in range(D_FF // MXU_COLS):
        c0 = c * MXU_COLS
        gate = conv_cols(0, c0)
        val = conv_cols(1, D_FF + c0)
        a_ref[:, c0:c0 + MXU_COLS] = (jax.nn.gelu(gate, approximate=True) * val).astype(BF16)


def _up_glu(h, w, cw, cb, seq):
    n = h.shape[0]
    const = lambda shape: pl.BlockSpec(shape, lambda i: (0, 0))
    return pl.pallas_call(
        functools.partial(_up_glu_kernel, tiles_per_seq=seq // TM_PROJ),
        grid=(n // TM_PROJ,),
        in_specs=[pl.BlockSpec((TM_PROJ, D_MODEL), lambda i: (i, 0)),
                  const(w.shape), const(cw.shape), const(cb.shape)],
        out_specs=pl.BlockSpec((TM_PROJ, D_FF), lambda i: (i, 0)),
        out_shape=jax.ShapeDtypeStruct((n, D_FF), BF16),
        scratch_shapes=[
            pltpu.VMEM((SUBLANES, 2 * D_FF), F32),
            pltpu.VMEM((2, TM_PROJ + SUBLANES, MXU_COLS), F32),
        ],
        compiler_params=_params("arbitrary"),
        name="up_glu",
    )(h, w, cw, cb)


def _down_proj_kernel(a_ref, x_ref, w_ref, g_ref, xo_ref):
    y = jnp.dot(a_ref[...], w_ref[...], preferred_element_type=F32)
    xo_ref[...] = x_ref[...] + _rms(y, g_ref[...])


def _down_proj(a, x, w, g):
    n = x.shape[0]
    row = lambda width: pl.BlockSpec((TM_PROJ, width), lambda i: (i, 0))
    const = lambda shape: pl.BlockSpec(shape, lambda i: (0, 0))
    return pl.pallas_call(
        _down_proj_kernel,
        grid=(n // TM_PROJ,),
        in_specs=[row(D_FF), row(D_MODEL), const(w.shape), const(g.shape)],
        out_specs=row(D_MODEL),
        out_shape=jax.ShapeDtypeStruct((n, D_MODEL), F32),
        compiler_params=_params("arbitrary"),
        name="down_proj",
    )(a, x, w, g)


def _rotary_tables(seq, rot_dim, theta, period):
    inv = jnp.float32(theta) ** (-jnp.arange(0, rot_dim, 2, dtype=F32) / rot_dim)
    ang = jnp.arange(seq, dtype=F32)[:, None] * inv[None, :]
    cos, sin = jnp.cos(ang), jnp.sin(ang)
    half = rot_dim // 2
    zh = jnp.zeros((seq, half), F32)
    zp = jnp.zeros((seq, period - rot_dim), F32)
    tabs = (jnp.concatenate([cos, cos, jnp.ones_like(zp)], axis=1),
            jnp.concatenate([zh, sin, zp], axis=1),
            jnp.concatenate([-sin, zh, zp], axis=1))
    return jnp.stack([jnp.tile(t, (1, LANES // period)) for t in tabs])


def _retention_tables():
    H, C = RET_HEADS, RET_CHUNK
    log_g = jnp.log(1.0 - 2.0 ** (-5.0 - jnp.arange(H, dtype=F32)))
    idx = jnp.arange(C, dtype=F32)
    diff = idx[:, None] - idx[None, :]
    decay = jnp.where(diff >= 0, jnp.exp(jnp.maximum(diff, 0.0) * log_g[:, None, None]), 0.0)
    xi = jnp.exp((idx + 1.0) * log_g[:, None])
    zeta = jnp.exp((C - 1.0 - idx) * log_g[:, None])
    chunk_decay = jnp.exp(C * log_g)
    xi_t = jnp.repeat(xi.T, RET_QK_DIM, axis=1)
    zeta_t = jnp.repeat(zeta.T, RET_QK_DIM, axis=1)
    row_head = jnp.arange(LANES) // RET_QK_DIM
    col_head = jnp.arange(RET_WIDTH) // RET_V_DIM
    bd = (row_head[:, None] == col_head[None, :]).astype(F32)
    cd = bd * chunk_decay[row_head][:, None]
    avg = ((col_head[:, None] == col_head[None, :]).astype(F32) / RET_V_DIM).astype(BF16)
    win = jnp.repeat(jnp.asarray(POOL_WINDOWS, F32), POOL_DIM)[None, :]
    return decay, xi_t, zeta_t, cd, bd, avg, win


def _block_diag(w):
    G, P, _ = w.shape
    eye = jnp.eye(G, dtype=w.dtype)
    return (eye[:, None, :, None] * w[:, :, None, :]).reshape(G * P, G * P)


def kernel(x, norm_mix_pre, norm_mix_post, w_in, lambda_q1, lambda_k1, lambda_q2, lambda_k2, diff_subln,
           ret_norm, pool_w, pool_scale, w_out, norm_mlp_pre, norm_mlp_post, w_up, conv_w, conv_b, w_down):
    batch, seq, _ = x.shape
    depth = w_in.shape[0]
    assert seq % TQ == 0 and seq % TM_PROJ == 0 and seq % R_RET == 0
    xf = x.reshape(batch * seq, D_MODEL)

    tda = _rotary_tables(seq, DA_ROT_DIM, ROPE_THETA, DA_QK_DIM)
    tret = _rotary_tables(seq, RET_QK_DIM, RET_THETA, RET_QK_DIM)
    rtabs = _retention_tables()
    row = lambda a: a.reshape(1, -1)

    for l in range(depth):
        lam_init = 0.8 - 0.6 * math.exp(-0.3 * l)
        proj = _in_proj(xf, row(norm_mix_pre[l]), w_in[l].astype(BF16), tda, tret, seq)
        o_da = _diff_attn(proj, row(lambda_q1[l]), row(lambda_k1[l]), row(lambda_q2[l]), row(lambda_k2[l]),
                          row(diff_subln[l]), lam_init, batch, seq)
        o_rp = _ret_pool(proj, rtabs, row(ret_norm[l]), _block_diag(pool_w[l]).astype(BF16),
                         row(pool_scale[l]), batch, seq)
        xf, h = _out_proj(o_da, o_rp, xf, w_out[l].astype(BF16), row(norm_mix_post[l]), row(norm_mlp_pre[l]))
        a = _up_glu(h, w_up[l].astype(BF16), conv_w[l], row(conv_b[l]), seq)
        xf = _down_proj(a, xf, w_down[l].astype(BF16), row(norm_mlp_post[l]))
    return xf.reshape(batch, seq, D_MODEL)
```

```python
import functools
import math

import jax
import jax.numpy as jnp
from jax import lax
from jax.experimental import pallas as pl
from jax.experimental.pallas import tpu as pltpu

F32 = jnp.float32
BF16 = jnp.bfloat16

D_MODEL = 1024
DA_HEADS = 4
DA_QK_DIM = 64
DA_V_DIM = 128
DA_ROT_DIM = 16
ROPE_THETA = 500000.0
RET_HEADS = 4
RET_QK_DIM = 32
RET_V_DIM = 64
RET_THETA = 10000.0
RET_CHUNK = 128
POOL_GROUPS = 4
POOL_DIM = 64
POOL_WINDOWS = (2, 4, 8, 16)
POOL_HALO = 16
DA_WIDTH = 512
RET_WIDTH = 256
POOL_WIDTH = 256
IN_WIDTH = 2560
D_FF = 2816
CONV_WIDTH = 3
EPS = 1e-6

COL_Q_DA, COL_K_DA, COL_V_DA = 0, 512, 1024
COL_QK_R, COL_V_R, COL_G_R, COL_U = 1536, 1792, 2048, 2304

LANES = 128
SUBLANES = 8
MXU_COLS = 256
VMEM_LIMIT = 56 * 1024 * 1024

TM_PROJ = 512
TQ = 1024
TK = 512
SUB = 256
QK_AHEAD = 2
VT_ROWS = DA_V_DIM + 16
LOG2E = math.log2(math.e)
GELU_C = math.sqrt(2.0 / math.pi)
R_RET = 512


def _params(*sem):
    return pltpu.CompilerParams(dimension_semantics=sem, vmem_limit_bytes=VMEM_LIMIT)


def _layer_spec(arr, layer):
    tail = arr.shape[1:]
    return pl.BlockSpec((None,) + tail, lambda *_: (layer,) + (0,) * len(tail))


def _rms(x, g):
    return x * lax.rsqrt(jnp.mean(x * x, axis=-1, keepdims=True) + EPS) * g


def _rot(x, tab_ref, shift):
    return (x * tab_ref[0]
            + pltpu.roll(x, shift, 1) * tab_ref[1]
            + pltpu.roll(x, LANES - shift, 1) * tab_ref[2])


def _in_proj_kernel(x_ref, g_ref, w_ref, tda_ref, tret_ref, o_ref):
    h = _rms(x_ref[...], g_ref[...]).astype(BF16)
    for j in range(IN_WIDTH // MXU_COLS):
        c0 = j * MXU_COLS
        p = jnp.dot(h, w_ref[:, c0:c0 + MXU_COLS], preferred_element_type=F32)
        lo, hi = p[:, :LANES], p[:, LANES:]
        if c0 < COL_V_DA:
            lo = _rot(lo, tda_ref, DA_ROT_DIM // 2)
            hi = _rot(hi, tda_ref, DA_ROT_DIM // 2)
            if c0 < COL_K_DA:
                lo = lo * (DA_QK_DIM ** -0.5 * LOG2E)
                hi = hi * (DA_QK_DIM ** -0.5 * LOG2E)
        elif c0 == COL_QK_R:
            lo = _rot(lo, tret_ref, RET_QK_DIM // 2)
            hi = _rot(hi, tret_ref, RET_QK_DIM // 2) * (RET_QK_DIM ** -0.5)
        o_ref[:, c0:c0 + LANES] = lo.astype(BF16)
        o_ref[:, c0 + LANES:c0 + MXU_COLS] = hi.astype(BF16)


def _in_proj(x, g, w, tda, tret, seq, layer):
    n = x.shape[0]
    tiles_per_seq = seq // TM_PROJ
    return pl.pallas_call(
        _in_proj_kernel,
        grid=(n // TM_PROJ,),
        in_specs=[
            pl.BlockSpec((TM_PROJ, D_MODEL), lambda i: (i, 0)),
            _layer_spec(g, layer),
            _layer_spec(w, layer),
            pl.BlockSpec((3, TM_PROJ, LANES), lambda i: (0, i % tiles_per_seq, 0)),
            pl.BlockSpec((3, TM_PROJ, LANES), lambda i: (0, i % tiles_per_seq, 0)),
        ],
        out_specs=pl.BlockSpec((TM_PROJ, IN_WIDTH), lambda i: (i, 0)),
        out_shape=jax.ShapeDtypeStruct((n, IN_WIDTH), BF16),
        compiler_params=_params("arbitrary"),
        name="in_proj",
    )(x, g, w, tda, tret)


def _diff_attn_kernel(q_ref, k_ref, v_ref, lq1_ref, lk1_ref, lq2_ref, lk2_ref, g_ref, o_ref,
                      vt_ref, qt_ref, s0_ref, s1_ref, mx0_ref, mx1_ref, m_ref, acc_ref, *, lam_init):
    qi = pl.program_id(2)
    nkv = v_ref.shape[0] // TK
    s_refs, mx_refs = (s0_ref, s1_ref), (mx0_ref, mx1_ref)

    @pl.when(qi == 0)
    def _():
        for c in range(nkv):
            vt_ref[c, :DA_V_DIM, :] = v_ref[c * TK:(c + 1) * TK, :].astype(F32).T.astype(BF16)
            vt_ref[c, DA_V_DIM:, :] = jnp.ones((VT_ROWS - DA_V_DIM, TK), BF16)

    qt = q_ref[...].astype(F32).T
    feat = lax.broadcasted_iota(jnp.int32, qt.shape, 0)
    qt_ref[0] = jnp.where(feat < DA_QK_DIM, qt, 0.0).astype(BF16)
    qt_ref[1] = jnp.where(feat >= DA_QK_DIM, qt, 0.0).astype(BF16)

    m_ref[...] = jnp.full(m_ref.shape, -jnp.inf, F32)
    acc_ref[...] = jnp.zeros(acc_ref.shape, F32)

    def columns(diag):
        out = []
        for qb in range(TQ // SUB):
            if diag is not None and diag * TK > (qb + 1) * SUB - 1:
                continue
            mask = diag is not None and (diag + 1) * TK - 1 > qb * SUB
            out += [(2 * qb + mi, qb, mi, mask) for mi in range(2)]
        return out

    def score(c, slot, diag):
        def one(j, qb, mi, mask):
            ks = k_ref[pl.ds(pl.multiple_of(c * TK, TK), TK), :]
            st = jnp.dot(ks, qt_ref[mi, :, qb * SUB:(qb + 1) * SUB],
                         preferred_element_type=F32)
            if mask:
                key = diag * TK + lax.broadcasted_iota(jnp.int32, st.shape, 0)
                qry = qb * SUB + lax.broadcasted_iota(jnp.int32, st.shape, 1)
                st = jnp.where(key <= qry, st, -jnp.inf)
            s_refs[slot][j] = st
            mx_refs[slot][j] = jnp.max(st, axis=0, keepdims=True)
        return [functools.partial(one, *col) for col in columns(diag)]

    def update(c, slot, diag):
        def one(j, qb, mi, mask):
            qs = slice(qb * SUB, (qb + 1) * SUB)
            m_prev = m_ref[mi, :, qs]
            m_new = jnp.maximum(m_prev, mx_refs[slot][j])
            alpha = jnp.exp2(m_prev - m_new)
            p = jnp.exp2(s_refs[slot][j] - m_new).astype(BF16)
            acc_ref[mi, :, qs] = (acc_ref[mi, :, qs] * alpha
                                  + jnp.dot(vt_ref[c], p, preferred_element_type=F32))
            m_ref[mi, :, qs] = m_new
        return [functools.partial(one, *col) for col in columns(diag)]

    def emit(scores, updates):
        scores = list(scores)
        for s in scores[:QK_AHEAD]:
            s()
        rest = scores[QK_AHEAD:]
        for u in updates:
            u()
            if rest:
                rest.pop(0)()
        for s in rest:
            s()

    assert TQ // TK == 2
    c0 = 2 * qi

    @pl.when(qi > 0)
    def _():
        emit(score(0, 0, None), [])

        def body(i, carry):
            emit(score(2 * i + 1, 1, None), update(2 * i, 0, None))
            emit(score(2 * i + 2, 0, None), update(2 * i + 1, 1, None))
            return carry

        lax.fori_loop(0, qi - 1, body, 0)
        emit(score(c0 - 1, 1, None), update(c0 - 2, 0, None))
        emit(score(c0, 0, 0), update(c0 - 1, 1, None))

    @pl.when(qi == 0)
    def _():
        emit(score(c0, 0, 0), [])

    emit(score(c0 + 1, 1, 1), update(c0, 0, 0))
    emit([], update(c0 + 1, 1, 1))

    lam = (jnp.exp(jnp.sum(lq1_ref[...] * lk1_ref[...], axis=-1, keepdims=True))
           - jnp.exp(jnp.sum(lq2_ref[...] * lk2_ref[...], axis=-1, keepdims=True)) + lam_init)
    o1 = acc_ref[0, :DA_V_DIM, :] / acc_ref[0, DA_V_DIM:DA_V_DIM + 1, :]
    o2 = acc_ref[1, :DA_V_DIM, :] / acc_ref[1, DA_V_DIM:DA_V_DIM + 1, :]
    a = (o1 - lam * o2).T
    o_ref[...] = (_rms(a, g_ref[...]) * (1.0 - lam_init)).astype(BF16)


def _diff_attn(proj, lq1, lk1, lq2, lk2, g, lam_init, batch, seq, layer):
    n = proj.shape[0]
    nq = seq // TQ
    kcol, vcol = COL_K_DA // LANES, COL_V_DA // LANES
    return pl.pallas_call(
        functools.partial(_diff_attn_kernel, lam_init=lam_init),
        grid=(batch, DA_HEADS, nq),
        in_specs=[
            pl.BlockSpec((TQ, LANES), lambda b, h, i: (b * nq + i, h)),
            pl.BlockSpec((seq, LANES), lambda b, h, i: (b, kcol + h)),
            pl.BlockSpec((seq, LANES), lambda b, h, i: (b, vcol + h)),
            _layer_spec(lq1, layer), _layer_spec(lk1, layer), _layer_spec(lq2, layer), _layer_spec(lk2, layer),
            _layer_spec(g, layer),
        ],
        out_specs=pl.BlockSpec((TQ, LANES), lambda b, h, i: (b * nq + i, h)),
        out_shape=jax.ShapeDtypeStruct((n, DA_WIDTH), BF16),
        scratch_shapes=[
            pltpu.VMEM((seq // TK, VT_ROWS, TK), BF16),
            pltpu.VMEM((2, LANES, TQ), BF16),
            pltpu.VMEM((2 * TQ // SUB, TK, SUB), F32),
            pltpu.VMEM((2 * TQ // SUB, TK, SUB), F32),
            pltpu.VMEM((2 * TQ // SUB, 1, SUB), F32),
            pltpu.VMEM((2 * TQ // SUB, 1, SUB), F32),
            pltpu.VMEM((2, 1, TQ), F32),
            pltpu.VMEM((2, VT_ROWS, TQ), F32),
        ],
        compiler_params=_params("arbitrary", "arbitrary", "arbitrary"),
        name="diff_attn",
    )(proj, proj, proj, lq1, lk1, lq2, lk2, g)


def _split_dot(x, w):
    hi = x.astype(BF16)
    lo = (x - hi.astype(F32)).astype(BF16)
    return (jnp.dot(hi, w, preferred_element_type=F32) + jnp.dot(lo, w, preferred_element_type=F32))


def _ret_pool_kernel(qk_ref, v_ref, g_ref, u_ref, decay_ref, xi_ref, zeta_ref, cd_ref, bd_ref,
                     avg_ref, retg_ref, win_ref, pw_ref, ps_ref, o_ref,
                     state_ref, halo_ref, oret_ref):
    i = pl.program_id(1)

    @pl.when(i == 0)
    def _():
        state_ref[...] = jnp.zeros(state_ref.shape, F32)
        halo_ref[...] = jnp.zeros(halo_ref.shape, F32)

    C = RET_CHUNK
    chunks = range(R_RET // C)
    qlane = lax.broadcasted_iota(jnp.int32, (C, LANES), 1) // RET_QK_DIM
    vlane = lax.broadcasted_iota(jnp.int32, (C, RET_WIDTH), 1) // RET_V_DIM
    qs = [qk_ref[c * C:(c + 1) * C, :LANES] for c in chunks]
    ks = [qk_ref[c * C:(c + 1) * C, LANES:] for c in chunks]
    vs = [v_ref[c * C:(c + 1) * C, :] for c in chunks]
    inner = [[lax.dot_general(jnp.where(qlane == h, qs[c], jnp.zeros_like(qs[c])), ks[c],
                              (((1,), (1,)), ((), ())), preferred_element_type=F32)
              for h in range(RET_HEADS)] for c in chunks]
    upd = [lax.dot_general((ks[c].astype(F32) * zeta_ref[...]).astype(BF16), vs[c],
                           (((0,), (0,)), ((), ())), preferred_element_type=F32) for c in chunks]
    states = [state_ref[...]]
    for c in chunks:
        states.append(states[c] * cd_ref[...] + bd_ref[...] * upd[c])
    state_ref[...] = states[-1]
    for c in chunks:
        o_c = jnp.dot((qs[c].astype(F32) * xi_ref[...]).astype(BF16), states[c].astype(BF16),
                      preferred_element_type=F32)
        for h in range(RET_HEADS):
            vh = jnp.where(vlane == h, vs[c], jnp.zeros_like(vs[c]))
            o_c = o_c + jnp.dot((inner[c][h] * decay_ref[h]).astype(BF16), vh, preferred_element_type=F32)
        oret_ref[c * C:(c + 1) * C, :] = o_c

    o = oret_ref[...]
    mu = _split_dot(o, avg_ref[...])
    d = o - mu
    var = jnp.dot((d * d).astype(BF16), avg_ref[...], preferred_element_type=F32)
    y = d * lax.rsqrt(var + EPS) * retg_ref[...]
    gate = g_ref[...].astype(F32)
    o_ref[:, :RET_WIDTH] = (gate * jax.nn.sigmoid(gate) * y).astype(BF16)

    u = u_ref[...].astype(F32)
    halo_groups = POOL_HALO // SUBLANES
    ext = jnp.concatenate([halo_ref[...], u], axis=0).reshape(-1, SUBLANES, POOL_WIDTH)
    halo_ref[...] = u[R_RET - POOL_HALO:, :]

    def shift_down(x, k):
        padded = jnp.concatenate([jnp.zeros_like(x[:1]), x], axis=0)
        if k == SUBLANES:
            return padded[:-1]
        r = pltpu.roll(padded, k, 1)
        return jnp.where(lax.broadcasted_iota(jnp.int32, x.shape, 1) < k, r[:-1], r[1:])

    assert POOL_WINDOWS == (2, 4, 8, 16) and POOL_DIM * 2 == LANES
    s2 = ext + shift_down(ext, 1)
    s4 = s2 + shift_down(s2, 2)
    s4_hi = s4[:, :, LANES:]
    s8 = s4_hi + shift_down(s4_hi, 4)
    s16 = s8 + shift_down(s8, 8)
    first = lax.broadcasted_iota(jnp.int32, s8.shape, 2) < POOL_DIM
    psum = jnp.concatenate([jnp.where(first, s2[:, :, :LANES], s4[:, :, :LANES]), jnp.where(first, s8, s16)],
                           axis=2)[halo_groups:].reshape(R_RET, POOL_WIDTH)
    t = (i * R_RET + lax.broadcasted_iota(jnp.int32, u.shape, 0)).astype(F32)
    pooled = psum / jnp.minimum(t + 1.0, win_ref[...]) - u
    yp = jnp.dot(pooled.astype(BF16), pw_ref[...], preferred_element_type=F32) * ps_ref[...]
    o_ref[:, RET_WIDTH:] = yp.astype(BF16)


def _ret_pool(proj, tabs, ret_g, pool_w_bd, pool_scale, batch, seq, layer):
    n = proj.shape[0]
    nr = seq // R_RET
    decay, xi, zeta, cd, bd, avg, win = tabs

    def col(c):
        return pl.BlockSpec((R_RET, MXU_COLS), lambda b, i: (b * nr + i, c // MXU_COLS))

    def const(shape):
        return pl.BlockSpec(shape, lambda b, i: (0,) * len(shape))

    return pl.pallas_call(
        _ret_pool_kernel,
        grid=(batch, nr),
        in_specs=[
            col(COL_QK_R), col(COL_V_R), col(COL_G_R), col(COL_U),
            const(decay.shape), const(xi.shape), const(zeta.shape), const(cd.shape), const(bd.shape),
            const(avg.shape), _layer_spec(ret_g, layer), const(win.shape), _layer_spec(pool_w_bd, layer),
            _layer_spec(pool_scale, layer),
        ],
        out_specs=pl.BlockSpec((R_RET, RET_WIDTH + POOL_WIDTH), lambda b, i: (b * nr + i, 0)),
        out_shape=jax.ShapeDtypeStruct((n, RET_WIDTH + POOL_WIDTH), BF16),
        scratch_shapes=[
            pltpu.VMEM((LANES, RET_WIDTH), F32),
            pltpu.VMEM((POOL_HALO, POOL_WIDTH), F32),
            pltpu.VMEM((R_RET, RET_WIDTH), F32),
        ],
        compiler_params=_params("arbitrary", "arbitrary"),
        name="ret_pool",
    )(proj, proj, proj, proj, decay, xi, zeta, cd, bd, avg, ret_g, win, pool_w_bd, pool_scale)


def _out_proj_kernel(oda_ref, orp_ref, x_ref, w_ref, gpost_ref, gpre_ref, xo_ref, h_ref):
    mix = (jnp.dot(oda_ref[...], w_ref[:DA_WIDTH, :], preferred_element_type=F32)
           + jnp.dot(orp_ref[...], w_ref[DA_WIDTH:, :], preferred_element_type=F32))
    xn = x_ref[...] + _rms(mix, gpost_ref[...])
    xo_ref[...] = xn
    h_ref[...] = _rms(xn, gpre_ref[...]).astype(BF16)


def _out_proj(o_da, o_rp, x, w, g_post, g_pre, layer):
    n = x.shape[0]
    row = lambda width: pl.BlockSpec((TM_PROJ, width), lambda i: (i, 0))
    return pl.pallas_call(
        _out_proj_kernel,
        grid=(n // TM_PROJ,),
        in_specs=[row(DA_WIDTH), row(RET_WIDTH + POOL_WIDTH), row(D_MODEL),
                  _layer_spec(w, layer), _layer_spec(g_post, layer), _layer_spec(g_pre, layer)],
        out_specs=[row(D_MODEL), row(D_MODEL)],
        out_shape=[jax.ShapeDtypeStruct((n, D_MODEL), F32), jax.ShapeDtypeStruct((n, D_MODEL), BF16)],
        compiler_params=_params("arbitrary"),
        name="out_proj",
    )(o_da, o_rp, x, w, g_post, g_pre)


def _up_glu_kernel(h_ref, w_ref, cw_ref, cb_ref, a_ref, carry_ref, ext_ref, *, tiles_per_seq):
    i = pl.program_id(0)
    tm = h_ref.shape[0]

    @pl.when(i % tiles_per_seq == 0)
    def _():
        carry_ref[...] = jnp.zeros(carry_ref.shape, F32)

    h = h_ref[...]

    def conv_cols(slot, c0):
        u = jnp.dot(h, w_ref[:, c0:c0 + MXU_COLS], preferred_element_type=F32)
        ext_ref[slot, :SUBLANES, :] = carry_ref[:, c0:c0 + MXU_COLS]
        ext_ref[slot, SUBLANES:, :] = u
        carry_ref[:, c0:c0 + MXU_COLS] = u[tm - SUBLANES:, :]
        cw = cw_ref[:, c0:c0 + MXU_COLS]
        return (cb_ref[:, c0:c0 + MXU_COLS]
                + cw[0:1] * ext_ref[slot, SUBLANES - 2:SUBLANES - 2 + tm, :]
                + cw[1:2] * ext_ref[slot, SUBLANES - 1:SUBLANES - 1 + tm, :]
                + cw[2:3] * u)

    for c in range(D_FF // MXU_COLS):
        c0 = c * MXU_COLS
        g = conv_cols(0, c0)
        v = conv_cols(1, D_FF + c0)
        th = jnp.tanh(g * (g * g * (GELU_C * 0.044715) + GELU_C))
        a_ref[:, c0:c0 + MXU_COLS] = (g * v * (th + 1.0)).astype(BF16)


def _up_glu(h, w, cw, cb, seq, layer):
    n = h.shape[0]
    return pl.pallas_call(
        functools.partial(_up_glu_kernel, tiles_per_seq=seq // TM_PROJ),
        grid=(n // TM_PROJ,),
        in_specs=[pl.BlockSpec((TM_PROJ, D_MODEL), lambda i: (i, 0)),
                  _layer_spec(w, layer), _layer_spec(cw, layer), _layer_spec(cb, layer)],
        out_specs=pl.BlockSpec((TM_PROJ, D_FF), lambda i: (i, 0)),
        out_shape=jax.ShapeDtypeStruct((n, D_FF), BF16),
        scratch_shapes=[
            pltpu.VMEM((SUBLANES, 2 * D_FF), F32),
            pltpu.VMEM((2, TM_PROJ + SUBLANES, MXU_COLS), F32),
        ],
        compiler_params=_params("arbitrary"),
        name="up_glu",
    )(h, w, cw, cb)


def _down_proj_kernel(a_ref, x_ref, w_ref, g_ref, xo_ref):
    y = jnp.dot(a_ref[...], w_ref[...], preferred_element_type=F32)
    xo_ref[...] = x_ref[...] + _rms(y, g_ref[...])


def _down_proj(a, x, w, g, layer):
    n = x.shape[0]
    row = lambda width: pl.BlockSpec((TM_PROJ, width), lambda i: (i, 0))
    return pl.pallas_call(
        _down_proj_kernel,
        grid=(n // TM_PROJ,),
        in_specs=[row(D_FF), row(D_MODEL), _layer_spec(w, layer), _layer_spec(g, layer)],
        out_specs=row(D_MODEL),
        out_shape=jax.ShapeDtypeStruct((n, D_MODEL), F32),
        compiler_params=_params("arbitrary"),
        name="down_proj",
    )(a, x, w, g)


def _rotary_tables(seq, rot_dim, theta, period):
    inv = jnp.float32(theta) ** (-jnp.arange(0, rot_dim, 2, dtype=F32) / rot_dim)
    ang = jnp.arange(seq, dtype=F32)[:, None] * inv[None, :]
    cos, sin = jnp.cos(ang), jnp.sin(ang)
    half = rot_dim // 2
    zh = jnp.zeros((seq, half), F32)
    zp = jnp.zeros((seq, period - rot_dim), F32)
    tabs = (jnp.concatenate([cos, cos, jnp.ones_like(zp)], axis=1),
            jnp.concatenate([zh, sin, zp], axis=1),
            jnp.concatenate([-sin, zh, zp], axis=1))
    return jnp.stack([jnp.tile(t, (1, LANES // period)) for t in tabs])


def _retention_tables():
    H, C = RET_HEADS, RET_CHUNK
    log_g = jnp.log(1.0 - 2.0 ** (-5.0 - jnp.arange(H, dtype=F32)))
    idx = jnp.arange(C, dtype=F32)
    diff = idx[:, None] - idx[None, :]
    decay = jnp.where(diff >= 0, jnp.exp(jnp.maximum(diff, 0.0) * log_g[:, None, None]), 0.0)
    xi = jnp.exp((idx + 1.0) * log_g[:, None])
    zeta = jnp.exp((C - 1.0 - idx) * log_g[:, None])
    chunk_decay = jnp.exp(C * log_g)
    xi_t = jnp.repeat(xi.T, RET_QK_DIM, axis=1)
    zeta_t = jnp.repeat(zeta.T, RET_QK_DIM, axis=1)
    row_head = jnp.arange(LANES) // RET_QK_DIM
    col_head = jnp.arange(RET_WIDTH) // RET_V_DIM
    bd = (row_head[:, None] == col_head[None, :]).astype(F32)
    cd = bd * chunk_decay[row_head][:, None]
    avg = ((col_head[:, None] == col_head[None, :]).astype(F32) / RET_V_DIM).astype(BF16)
    win = jnp.repeat(jnp.asarray(POOL_WINDOWS, F32), POOL_DIM)[None, :]
    return decay, xi_t, zeta_t, cd, bd, avg, win


def _block_diag(w):
    L, G, P, _ = w.shape
    eye = jnp.eye(G, dtype=w.dtype)
    return (eye[None, :, None, :, None] * w[:, :, :, None, :]).reshape(L, G * P, G * P)


def kernel(x, norm_mix_pre, norm_mix_post, w_in, lambda_q1, lambda_k1, lambda_q2, lambda_k2, diff_subln,
           ret_norm, pool_w, pool_scale, w_out, norm_mlp_pre, norm_mlp_post, w_up, conv_w, conv_b, w_down):
    batch, seq, _ = x.shape
    depth = w_in.shape[0]
    assert seq % TQ == 0 and seq % TM_PROJ == 0 and seq % R_RET == 0
    xf = x.reshape(batch * seq, D_MODEL)

    tda = _rotary_tables(seq, DA_ROT_DIM, ROPE_THETA, DA_QK_DIM)
    tret = _rotary_tables(seq, RET_QK_DIM, RET_THETA, RET_QK_DIM)
    rtabs = _retention_tables()

    rows = lambda a: a.reshape(depth, 1, -1)
    w_in_b, w_out_b, w_up_b, w_down_b = (w.astype(BF16) for w in (w_in, w_out, w_up, w_down))
    pool_w_b = _block_diag(pool_w).astype(BF16)
    half_val = jnp.concatenate([jnp.ones((D_FF,), F32), jnp.full((D_FF,), 0.5, F32)])
    conv_w_s, conv_b_s = conv_w * half_val, rows(conv_b * half_val)
    g_mix_pre, g_mix_post, g_mlp_pre, g_mlp_post = (rows(g) for g in (norm_mix_pre, norm_mix_post,
                                                                      norm_mlp_pre, norm_mlp_post))
    lq1, lk1, lq2, lk2 = (rows(v) for v in (lambda_q1, lambda_k1, lambda_q2, lambda_k2))
    g_sub, g_ret, p_scale = rows(diff_subln), rows(ret_norm), rows(pool_scale)

    for l in range(depth):
        lam_init = 0.8 - 0.6 * math.exp(-0.3 * l)
        proj = _in_proj(xf, g_mix_pre, w_in_b, tda, tret, seq, l)
        o_da = _diff_attn(proj, lq1, lk1, lq2, lk2, g_sub, lam_init, batch, seq, l)
        o_rp = _ret_pool(proj, rtabs, g_ret, pool_w_b, p_scale, batch, seq, l)
        xf, h = _out_proj(o_da, o_rp, xf, w_out_b, g_mix_post, g_mlp_pre, l)
        a = _up_glu(h, w_up_b, conv_w_s, conv_b_s, seq, l)
        xf = _down_proj(a, xf, w_down_b, g_mlp_post, l)
    return xf.reshape(batch, seq, D_MODEL)
```

```python
import functools
import math

import jax
import jax.numpy as jnp
from jax import lax
from jax.experimental import pallas as pl
from jax.experimental.pallas import tpu as pltpu

F32 = jnp.float32
BF16 = jnp.bfloat16

D_MODEL = 1024
DA_HEADS = 4
DA_QK_DIM = 64
DA_V_DIM = 128
DA_ROT_DIM = 16
ROPE_THETA = 500000.0
RET_HEADS = 4
RET_QK_DIM = 32
RET_V_DIM = 64
RET_THETA = 10000.0
RET_CHUNK = 128
POOL_GROUPS = 4
POOL_DIM = 64
POOL_WINDOWS = (2, 4, 8, 16)
POOL_HALO = 16
DA_WIDTH = 512
RET_WIDTH = 256
POOL_WIDTH = 256
IN_WIDTH = 2560
D_FF = 2816
CONV_WIDTH = 3
EPS = 1e-6

COL_Q_DA, COL_K_DA, COL_V_DA = 0, 512, 1024
COL_QK_R, COL_V_R, COL_G_R, COL_U = 1536, 1792, 2048, 2304

LANES = 128
SUBLANES = 8
MXU_COLS = 256
VMEM_LIMIT = 56 * 1024 * 1024

TM_PROJ = 512
TQ = 1024
TK = 512
SUB = 256
QK_AHEAD = 2
VT_ROWS = DA_V_DIM + 16
LOG2E = math.log2(math.e)
GELU_C = math.sqrt(2.0 / math.pi)
R_RET = 512


def _params(*sem):
    return pltpu.CompilerParams(dimension_semantics=sem, vmem_limit_bytes=VMEM_LIMIT)


def _layer_spec(arr, layer):
    tail = arr.shape[1:]
    return pl.BlockSpec((None,) + tail, lambda *_: (layer,) + (0,) * len(tail))


def _rms(x, g):
    return x * lax.rsqrt(jnp.mean(x * x, axis=-1, keepdims=True) + EPS) * g


def _rot(x, tab_ref, shift):
    return (x * tab_ref[0]
            + pltpu.roll(x, shift, 1) * tab_ref[1]
            + pltpu.roll(x, LANES - shift, 1) * tab_ref[2])


def _in_proj_kernel(x_ref, g_ref, w_ref, tda_ref, tret_ref, o_ref):
    h = _rms(x_ref[...], g_ref[...]).astype(BF16)
    for j in range(IN_WIDTH // MXU_COLS):
        c0 = j * MXU_COLS
        p = jnp.dot(h, w_ref[:, c0:c0 + MXU_COLS], preferred_element_type=F32)
        lo, hi = p[:, :LANES], p[:, LANES:]
        if c0 < COL_V_DA:
            lo = _rot(lo, tda_ref, DA_ROT_DIM // 2)
            hi = _rot(hi, tda_ref, DA_ROT_DIM // 2)
            if c0 < COL_K_DA:
                lo = lo * (DA_QK_DIM ** -0.5 * LOG2E)
                hi = hi * (DA_QK_DIM ** -0.5 * LOG2E)
        elif c0 == COL_QK_R:
            lo = _rot(lo, tret_ref, RET_QK_DIM // 2)
            hi = _rot(hi, tret_ref, RET_QK_DIM // 2) * (RET_QK_DIM ** -0.5)
        o_ref[:, c0:c0 + LANES] = lo.astype(BF16)
        o_ref[:, c0 + LANES:c0 + MXU_COLS] = hi.astype(BF16)


def _in_proj(x, g, w, tda, tret, seq, layer):
    n = x.shape[0]
    tiles_per_seq = seq // TM_PROJ
    return pl.pallas_call(
        _in_proj_kernel,
        grid=(n // TM_PROJ,),
        in_specs=[
            pl.BlockSpec((TM_PROJ, D_MODEL), lambda i: (i, 0)),
            _layer_spec(g, layer),
            _layer_spec(w, layer),
            pl.BlockSpec((3, TM_PROJ, LANES), lambda i: (0, i % tiles_per_seq, 0)),
            pl.BlockSpec((3, TM_PROJ, LANES), lambda i: (0, i % tiles_per_seq, 0)),
        ],
        out_specs=pl.BlockSpec((TM_PROJ, IN_WIDTH), lambda i: (i, 0)),
        out_shape=jax.ShapeDtypeStruct((n, IN_WIDTH), BF16),
        compiler_params=_params("arbitrary"),
        name="in_proj",
    )(x, g, w, tda, tret)


def _diff_attn_kernel(q_ref, k_ref, v_ref, lq1_ref, lk1_ref, lq2_ref, lk2_ref, g_ref, o_ref,
                      vt_ref, qt_ref, s0_ref, s1_ref, mx0_ref, mx1_ref, m_ref, acc_ref, *, lam_init):
    qi = pl.program_id(2)
    nkv = v_ref.shape[0] // TK
    s_refs, mx_refs = (s0_ref, s1_ref), (mx0_ref, mx1_ref)

    @pl.when(qi == 0)
    def _():
        for c in range(nkv):
            vt_ref[c, :DA_V_DIM, :] = v_ref[c * TK:(c + 1) * TK, :].astype(F32).T.astype(BF16)
            vt_ref[c, DA_V_DIM:, :] = jnp.ones((VT_ROWS - DA_V_DIM, TK), BF16)

    qt = q_ref[...].astype(F32).T
    feat = lax.broadcasted_iota(jnp.int32, qt.shape, 0)
    qt_ref[0] = jnp.where(feat < DA_QK_DIM, qt, 0.0).astype(BF16)
    qt_ref[1] = jnp.where(feat >= DA_QK_DIM, qt, 0.0).astype(BF16)

    m_ref[...] = jnp.full(m_ref.shape, -jnp.inf, F32)
    acc_ref[...] = jnp.zeros(acc_ref.shape, F32)

    def columns(diag):
        out = []
        for qb in range(TQ // SUB):
            if diag is not None and diag * TK > (qb + 1) * SUB - 1:
                continue
            mask = diag is not None and (diag + 1) * TK - 1 > qb * SUB
            out += [(2 * qb + mi, qb, mi, mask) for mi in range(2)]
        return out

    def score(c, slot, diag):
        def one(j, qb, mi, mask):
            ks = k_ref[pl.ds(pl.multiple_of(c * TK, TK), TK), :]
            st = jnp.dot(ks, qt_ref[mi, :, qb * SUB:(qb + 1) * SUB],
                         preferred_element_type=F32)
            if mask:
                key = diag * TK + lax.broadcasted_iota(jnp.int32, st.shape, 0)
                qry = qb * SUB + lax.broadcasted_iota(jnp.int32, st.shape, 1)
                st = jnp.where(key <= qry, st, -jnp.inf)
            s_refs[slot][j] = st
            mx_refs[slot][j] = jnp.max(st, axis=0, keepdims=True)
        return [functools.partial(one, *col) for col in columns(diag)]

    def update(c, slot, diag):
        def one(j, qb, mi, mask):
            qs = slice(qb * SUB, (qb + 1) * SUB)
            m_prev = m_ref[mi, :, qs]
            m_new = jnp.maximum(m_prev, mx_refs[slot][j])
            alpha = jnp.exp2(m_prev - m_new)
            p = jnp.exp2(s_refs[slot][j] - m_new).astype(BF16)
            acc_ref[mi, :, qs] = (acc_ref[mi, :, qs] * alpha
                                  + jnp.dot(vt_ref[c], p, preferred_element_type=F32))
            m_ref[mi, :, qs] = m_new
        return [functools.partial(one, *col) for col in columns(diag)]

    def emit(scores, updates):
        scores = list(scores)
        for s in scores[:QK_AHEAD]:
            s()
        rest = scores[QK_AHEAD:]
        for u in updates:
            u()
            if rest:
                rest.pop(0)()
        for s in rest:
            s()

    assert TQ // TK == 2
    c0 = 2 * qi

    @pl.when(qi > 0)
    def _():
        emit(score(0, 0, None), [])

        def body(i, carry):
            emit(score(2 * i + 1, 1, None), update(2 * i, 0, None))
            emit(score(2 * i + 2, 0, None), update(2 * i + 1, 1, None))
            return carry

        lax.fori_loop(0, qi - 1, body, 0)
        emit(score(c0 - 1, 1, None), update(c0 - 2, 0, None))
        emit(score(c0, 0, 0), update(c0 - 1, 1, None))

    @pl.when(qi == 0)
    def _():
        emit(score(c0, 0, 0), [])

    emit(score(c0 + 1, 1, 1), update(c0, 0, 0))
    emit([], update(c0 + 1, 1, 1))

    lam = (jnp.exp(jnp.sum(lq1_ref[...] * lk1_ref[...], axis=-1, keepdims=True))
           - jnp.exp(jnp.sum(lq2_ref[...] * lk2_ref[...], axis=-1, keepdims=True)) + lam_init)
    o1 = acc_ref[0, :DA_V_DIM, :] / acc_ref[0, DA_V_DIM:DA_V_DIM + 1, :]
    o2 = acc_ref[1, :DA_V_DIM, :] / acc_ref[1, DA_V_DIM:DA_V_DIM + 1, :]
    a = (o1 - lam * o2).T
    o_ref[...] = (_rms(a, g_ref[...]) * (1.0 - lam_init)).astype(BF16)


def _diff_attn(proj, lq1, lk1, lq2, lk2, g, lam_init, batch, seq, layer):
    n = proj.shape[0]
    nq = seq // TQ
    kcol, vcol = COL_K_DA // LANES, COL_V_DA // LANES
    return pl.pallas_call(
        functools.partial(_diff_attn_kernel, lam_init=lam_init),
        grid=(batch, DA_HEADS, nq),
        in_specs=[
            pl.BlockSpec((TQ, LANES), lambda b, h, i: (b * nq + i, h)),
            pl.BlockSpec((seq, LANES), lambda b, h, i: (b, kcol + h)),
            pl.BlockSpec((seq, LANES), lambda b, h, i: (b, vcol + h)),
            _layer_spec(lq1, layer), _layer_spec(lk1, layer), _layer_spec(lq2, layer), _layer_spec(lk2, layer),
            _layer_spec(g, layer),
        ],
        out_specs=pl.BlockSpec((TQ, LANES), lambda b, h, i: (b * nq + i, h)),
        out_shape=jax.ShapeDtypeStruct((n, DA_WIDTH), BF16),
        scratch_shapes=[
            pltpu.VMEM((seq // TK, VT_ROWS, TK), BF16),
            pltpu.VMEM((2, LANES, TQ), BF16),
            pltpu.VMEM((2 * TQ // SUB, TK, SUB), F32),
            pltpu.VMEM((2 * TQ // SUB, TK, SUB), F32),
            pltpu.VMEM((2 * TQ // SUB, 1, SUB), F32),
            pltpu.VMEM((2 * TQ // SUB, 1, SUB), F32),
            pltpu.VMEM((2, 1, TQ), F32),
            pltpu.VMEM((2, VT_ROWS, TQ), F32),
        ],
        compiler_params=_params("arbitrary", "arbitrary", "arbitrary"),
        name="diff_attn",
    )(proj, proj, proj, lq1, lk1, lq2, lk2, g)


def _split_dot(x, w):
    hi = x.astype(BF16)
    lo = (x - hi.astype(F32)).astype(BF16)
    return (jnp.dot(hi, w, preferred_element_type=F32) + jnp.dot(lo, w, preferred_element_type=F32))


def _ret_pool_kernel(qk_ref, v_ref, g_ref, u_ref, decay_ref, xi_ref, zeta_ref, cd_ref, bd_ref,
                     avg_ref, retg_ref, win_ref, pw_ref, ps_ref, o_ref,
                     state_ref, halo_ref, oret_ref):
    i = pl.program_id(1)

    @pl.when(i == 0)
    def _():
        state_ref[...] = jnp.zeros(state_ref.shape, F32)
        halo_ref[...] = jnp.zeros(halo_ref.shape, F32)

    C = RET_CHUNK
    chunks = range(R_RET // C)
    qlane = lax.broadcasted_iota(jnp.int32, (C, LANES), 1) // RET_QK_DIM
    vlane = lax.broadcasted_iota(jnp.int32, (C, RET_WIDTH), 1) // RET_V_DIM
    qs = [qk_ref[c * C:(c + 1) * C, :LANES] for c in chunks]
    ks = [qk_ref[c * C:(c + 1) * C, LANES:] for c in chunks]
    vs = [v_ref[c * C:(c + 1) * C, :] for c in chunks]
    inner = [[lax.dot_general(jnp.where(qlane == h, qs[c], jnp.zeros_like(qs[c])), ks[c],
                              (((1,), (1,)), ((), ())), preferred_element_type=F32)
              for h in range(RET_HEADS)] for c in chunks]
    upd = [lax.dot_general((ks[c].astype(F32) * zeta_ref[...]).astype(BF16), vs[c],
                           (((0,), (0,)), ((), ())), preferred_element_type=F32) for c in chunks]
    states = [state_ref[...]]
    for c in chunks:
        states.append(states[c] * cd_ref[...] + bd_ref[...] * upd[c])
    state_ref[...] = states[-1]
    for c in chunks:
        o_c = jnp.dot((qs[c].astype(F32) * xi_ref[...]).astype(BF16), states[c].astype(BF16),
                      preferred_element_type=F32)
        for h in range(RET_HEADS):
            vh = jnp.where(vlane == h, vs[c], jnp.zeros_like(vs[c]))
            o_c = o_c + jnp.dot((inner[c][h] * decay_ref[h]).astype(BF16), vh, preferred_element_type=F32)
        oret_ref[c * C:(c + 1) * C, :] = o_c

    o = oret_ref[...]
    mu = _split_dot(o, avg_ref[...])
    d = o - mu
    var = jnp.dot((d * d).astype(BF16), avg_ref[...], preferred_element_type=F32)
    y = d * lax.rsqrt(var + EPS) * retg_ref[...]
    gate = g_ref[...].astype(F32)
    o_ref[:, :RET_WIDTH] = (gate * jax.nn.sigmoid(gate) * y).astype(BF16)

    u = u_ref[...].astype(F32)
    halo_groups = POOL_HALO // SUBLANES
    ext = jnp.concatenate([halo_ref[...], u], axis=0).reshape(-1, SUBLANES, POOL_WIDTH)
    halo_ref[...] = u[R_RET - POOL_HALO:, :]

    def shift_down(x, k):
        padded = jnp.concatenate([jnp.zeros_like(x[:1]), x], axis=0)
        if k == SUBLANES:
            return padded[:-1]
        r = pltpu.roll(padded, k, 1)
        return jnp.where(lax.broadcasted_iota(jnp.int32, x.shape, 1) < k, r[:-1], r[1:])

    assert POOL_WINDOWS == (2, 4, 8, 16) and POOL_DIM * 2 == LANES
    s2 = ext + shift_down(ext, 1)
    s4 = s2 + shift_down(s2, 2)
    s4_hi = s4[:, :, LANES:]
    s8 = s4_hi + shift_down(s4_hi, 4)
    s16 = s8 + shift_down(s8, 8)
    first = lax.broadcasted_iota(jnp.int32, s8.shape, 2) < POOL_DIM
    psum = jnp.concatenate([jnp.where(first, s2[:, :, :LANES], s4[:, :, :LANES]), jnp.where(first, s8, s16)],
                           axis=2)[halo_groups:].reshape(R_RET, POOL_WIDTH)
    t = (i * R_RET + lax.broadcasted_iota(jnp.int32, u.shape, 0)).astype(F32)
    pooled = psum / jnp.minimum(t + 1.0, win_ref[...]) - u
    yp = jnp.dot(pooled.astype(BF16), pw_ref[...], preferred_element_type=F32) * ps_ref[...]
    o_ref[:, RET_WIDTH:] = yp.astype(BF16)


def _ret_pool(proj, tabs, ret_g, pool_w_bd, pool_scale, batch, seq, layer):
    n = proj.shape[0]
    nr = seq // R_RET
    decay, xi, zeta, cd, bd, avg, win = tabs

    def col(c):
        return pl.BlockSpec((R_RET, MXU_COLS), lambda b, i: (b * nr + i, c // MXU_COLS))

    def const(shape):
        return pl.BlockSpec(shape, lambda b, i: (0,) * len(shape))

    return pl.pallas_call(
        _ret_pool_kernel,
        grid=(batch, nr),
        in_specs=[
            col(COL_QK_R), col(COL_V_R), col(COL_G_R), col(COL_U),
            const(decay.shape), const(xi.shape), const(zeta.shape), const(cd.shape), const(bd.shape),
            const(avg.shape), _layer_spec(ret_g, layer), const(win.shape), _layer_spec(pool_w_bd, layer),
            _layer_spec(pool_scale, layer),
        ],
        out_specs=pl.BlockSpec((R_RET, RET_WIDTH + POOL_WIDTH), lambda b, i: (b * nr + i, 0)),
        out_shape=jax.ShapeDtypeStruct((n, RET_WIDTH + POOL_WIDTH), BF16),
        scratch_shapes=[
            pltpu.VMEM((LANES, RET_WIDTH), F32),
            pltpu.VMEM((POOL_HALO, POOL_WIDTH), F32),
            pltpu.VMEM((R_RET, RET_WIDTH), F32),
        ],
        compiler_params=_params("arbitrary", "arbitrary"),
        name="ret_pool",
    )(proj, proj, proj, proj, decay, xi, zeta, cd, bd, avg, ret_g, win, pool_w_bd, pool_scale)


def _mix_up_kernel(oda_ref, orp_ref, x_ref, wo_ref, gpost_ref, gpre_ref, w_ref, cw_ref, cb_ref,
                   xo_ref, a_ref, carry_ref, ext_ref, *, tiles_per_seq):
    i = pl.program_id(0)
    tm = x_ref.shape[0]

    @pl.when(i % tiles_per_seq == 0)
    def _():
        carry_ref[...] = jnp.zeros(carry_ref.shape, F32)

    mix = (jnp.dot(oda_ref[...], wo_ref[:DA_WIDTH, :], preferred_element_type=F32)
           + jnp.dot(orp_ref[...], wo_ref[DA_WIDTH:, :], preferred_element_type=F32))
    xn = x_ref[...] + _rms(mix, gpost_ref[...])
    xo_ref[...] = xn
    h = _rms(xn, gpre_ref[...]).astype(BF16)

    def conv_cols(slot, c0):
        u = jnp.dot(h, w_ref[:, c0:c0 + MXU_COLS], preferred_element_type=F32)
        ext_ref[slot, :SUBLANES, :] = carry_ref[:, c0:c0 + MXU_COLS]
        ext_ref[slot, SUBLANES:, :] = u
        carry_ref[:, c0:c0 + MXU_COLS] = u[tm - SUBLANES:, :]
        cw = cw_ref[:, c0:c0 + MXU_COLS]
        return (cb_ref[:, c0:c0 + MXU_COLS]
                + cw[0:1] * ext_ref[slot, SUBLANES - 2:SUBLANES - 2 + tm, :]
                + cw[1:2] * ext_ref[slot, SUBLANES - 1:SUBLANES - 1 + tm, :]
                + cw[2:3] * u)

    for c in range(D_FF // MXU_COLS):
        c0 = c * MXU_COLS
        g = conv_cols(0, c0)
        v = conv_cols(1, D_FF + c0)
        th = jnp.tanh(g * (g * g * (GELU_C * 0.044715) + GELU_C))
        a_ref[:, c0:c0 + MXU_COLS] = (g * v * (th + 1.0)).astype(BF16)


def _mix_up(o_da, o_rp, x, w_out, g_post, g_pre, w_up, cw, cb, seq, layer):
    n = x.shape[0]
    row = lambda width: pl.BlockSpec((TM_PROJ, width), lambda i: (i, 0))

    def resident(arr):
        tail = arr.shape[1:]
        return pl.BlockSpec((None,) + tail, lambda *_: (layer,) + (0,) * len(tail),
                            pipeline_mode=pl.Buffered(1))

    return pl.pallas_call(
        functools.partial(_mix_up_kernel, tiles_per_seq=seq // TM_PROJ),
        grid=(n // TM_PROJ,),
        in_specs=[row(DA_WIDTH), row(RET_WIDTH + POOL_WIDTH), row(D_MODEL),
                  resident(w_out), _layer_spec(g_post, layer), _layer_spec(g_pre, layer),
                  resident(w_up), _layer_spec(cw, layer), _layer_spec(cb, layer)],
        out_specs=[row(D_MODEL), row(D_FF)],
        out_shape=[jax.ShapeDtypeStruct((n, D_MODEL), F32), jax.ShapeDtypeStruct((n, D_FF), BF16)],
        scratch_shapes=[
            pltpu.VMEM((SUBLANES, 2 * D_FF), F32),
            pltpu.VMEM((2, TM_PROJ + SUBLANES, MXU_COLS), F32),
        ],
        compiler_params=_params("arbitrary"),
        name="mix_up",
    )(o_da, o_rp, x, w_out, g_post, g_pre, w_up, cw, cb)


def _down_proj_kernel(a_ref, x_ref, w_ref, g_ref, xo_ref):
    y = jnp.dot(a_ref[...], w_ref[...], preferred_element_type=F32)
    xo_ref[...] = x_ref[...] + _rms(y, g_ref[...])


def _down_proj(a, x, w, g, layer):
    n = x.shape[0]
    row = lambda width: pl.BlockSpec((TM_PROJ, width), lambda i: (i, 0))
    return pl.pallas_call(
        _down_proj_kernel,
        grid=(n // TM_PROJ,),
        in_specs=[row(D_FF), row(D_MODEL), _layer_spec(w, layer), _layer_spec(g, layer)],
        out_specs=row(D_MODEL),
        out_shape=jax.ShapeDtypeStruct((n, D_MODEL), F32),
        compiler_params=_params("arbitrary"),
        name="down_proj",
    )(a, x, w, g)


def _rotary_tables(seq, rot_dim, theta, period):
    inv = jnp.float32(theta) ** (-jnp.arange(0, rot_dim, 2, dtype=F32) / rot_dim)
    ang = jnp.arange(seq, dtype=F32)[:, None] * inv[None, :]
    cos, sin = jnp.cos(ang), jnp.sin(ang)
    half = rot_dim // 2
    zh = jnp.zeros((seq, half), F32)
    zp = jnp.zeros((seq, period - rot_dim), F32)
    tabs = (jnp.concatenate([cos, cos, jnp.ones_like(zp)], axis=1),
            jnp.concatenate([zh, sin, zp], axis=1),
            jnp.concatenate([-sin, zh, zp], axis=1))
    return jnp.stack([jnp.tile(t, (1, LANES // period)) for t in tabs])


def _retention_tables():
    H, C = RET_HEADS, RET_CHUNK
    log_g = jnp.log(1.0 - 2.0 ** (-5.0 - jnp.arange(H, dtype=F32)))
    idx = jnp.arange(C, dtype=F32)
    diff = idx[:, None] - idx[None, :]
    decay = jnp.where(diff >= 0, jnp.exp(jnp.maximum(diff, 0.0) * log_g[:, None, None]), 0.0)
    xi = jnp.exp((idx + 1.0) * log_g[:, None])
    zeta = jnp.exp((C - 1.0 - idx) * log_g[:, None])
    chunk_decay = jnp.exp(C * log_g)
    xi_t = jnp.repeat(xi.T, RET_QK_DIM, axis=1)
    zeta_t = jnp.repeat(zeta.T, RET_QK_DIM, axis=1)
    row_head = jnp.arange(LANES) // RET_QK_DIM
    col_head = jnp.arange(RET_WIDTH) // RET_V_DIM
    bd = (row_head[:, None] == col_head[None, :]).astype(F32)
    cd = bd * chunk_decay[row_head][:, None]
    avg = ((col_head[:, None] == col_head[None, :]).astype(F32) / RET_V_DIM).astype(BF16)
    win = jnp.repeat(jnp.asarray(POOL_WINDOWS, F32), POOL_DIM)[None, :]
    return decay, xi_t, zeta_t, cd, bd, avg, win


def _block_diag(w):
    L, G, P, _ = w.shape
    eye = jnp.eye(G, dtype=w.dtype)
    return (eye[None, :, None, :, None] * w[:, :, :, None, :]).reshape(L, G * P, G * P)


def kernel(x, norm_mix_pre, norm_mix_post, w_in, lambda_q1, lambda_k1, lambda_q2, lambda_k2, diff_subln,
           ret_norm, pool_w, pool_scale, w_out, norm_mlp_pre, norm_mlp_post, w_up, conv_w, conv_b, w_down):
    batch, seq, _ = x.shape
    depth = w_in.shape[0]
    assert seq % TQ == 0 and seq % TM_PROJ == 0 and seq % R_RET == 0
    xf = x.reshape(batch * seq, D_MODEL)

    tda = _rotary_tables(seq, DA_ROT_DIM, ROPE_THETA, DA_QK_DIM)
    tret = _rotary_tables(seq, RET_QK_DIM, RET_THETA, RET_QK_DIM)
    rtabs = _retention_tables()

    rows = lambda a: a.reshape(depth, 1, -1)
    w_in_b, w_out_b, w_up_b, w_down_b = (w.astype(BF16) for w in (w_in, w_out, w_up, w_down))
    pool_w_b = _block_diag(pool_w).astype(BF16)
    half_val = jnp.concatenate([jnp.ones((D_FF,), F32), jnp.full((D_FF,), 0.5, F32)])
    conv_w_s, conv_b_s = conv_w * half_val, rows(conv_b * half_val)
    g_mix_pre, g_mix_post, g_mlp_pre, g_mlp_post = (rows(g) for g in (norm_mix_pre, norm_mix_post,
                                                                      norm_mlp_pre, norm_mlp_post))
    lq1, lk1, lq2, lk2 = (rows(v) for v in (lambda_q1, lambda_k1, lambda_q2, lambda_k2))
    g_sub, g_ret, p_scale = rows(diff_subln), rows(ret_norm), rows(pool_scale)

    for l in range(depth):
        lam_init = 0.8 - 0.6 * math.exp(-0.3 * l)
        proj = _in_proj(xf, g_mix_pre, w_in_b, tda, tret, seq, l)
        o_da = _diff_attn(proj, lq1, lk1, lq2, lk2, g_sub, lam_init, batch, seq, l)
        o_rp = _ret_pool(proj, rtabs, g_ret, pool_w_b, p_scale, batch, seq, l)
        xf, a = _mix_up(o_da, o_rp, xf, w_out_b, g_mix_post, g_mlp_pre, w_up_b, conv_w_s, conv_b_s, seq, l)
        xf = _down_proj(a, xf, w_down_b, g_mlp_post, l)
    return xf.reshape(batch, seq, D_MODEL)
```

```python
import functools
import math

import jax
import jax.numpy as jnp
from jax import lax
from jax.experimental import pallas as pl
from jax.experimental.pallas import tpu as pltpu

F32 = jnp.float32
BF16 = jnp.bfloat16

D_MODEL = 1024
DA_HEADS = 4
DA_QK_DIM = 64
DA_V_DIM = 128
DA_ROT_DIM = 16
ROPE_THETA = 500000.0
RET_HEADS = 4
RET_QK_DIM = 32
RET_V_DIM = 64
RET_THETA = 10000.0
RET_CHUNK = 128
POOL_GROUPS = 4
POOL_DIM = 64
POOL_WINDOWS = (2, 4, 8, 16)
POOL_HALO = 16
DA_WIDTH = 512
RET_WIDTH = 256
POOL_WIDTH = 256
IN_WIDTH = 2560
D_FF = 2816
CONV_WIDTH = 3
EPS = 1e-6

COL_Q_DA, COL_K_DA, COL_V_DA = 0, 512, 1024
COL_QK_R, COL_V_R, COL_G_R, COL_U = 1536, 1792, 2048, 2304

LANES = 128
SUBLANES = 8
MXU_COLS = 256
VMEM_LIMIT = 56 * 1024 * 1024

TM_PROJ = 512
TQ = 1024
TK = 512
SUB = 256
QK_AHEAD = 2
VT_ROWS = DA_V_DIM + 16
LOG2E = math.log2(math.e)
GELU_C = math.sqrt(2.0 / math.pi)
R_RET = 512


def _params(*sem):
    return pltpu.CompilerParams(dimension_semantics=sem, vmem_limit_bytes=VMEM_LIMIT)


def _layer_spec(arr, layer):
    tail = arr.shape[1:]
    return pl.BlockSpec((None,) + tail, lambda *_: (layer,) + (0,) * len(tail))


def _rms(x, g):
    return x * lax.rsqrt(jnp.mean(x * x, axis=-1, keepdims=True) + EPS) * g


def _rot(x, tab_ref, shift):
    return (x * tab_ref[0]
            + pltpu.roll(x, shift, 1) * tab_ref[1]
            + pltpu.roll(x, LANES - shift, 1) * tab_ref[2])


def _in_proj_kernel(x_ref, g_ref, w_ref, tda_ref, tret_ref, o_ref):
    h = _rms(x_ref[...], g_ref[...]).astype(BF16)
    for j in range(IN_WIDTH // MXU_COLS):
        c0 = j * MXU_COLS
        p = jnp.dot(h, w_ref[:, c0:c0 + MXU_COLS], preferred_element_type=F32)
        lo, hi = p[:, :LANES], p[:, LANES:]
        if c0 < COL_V_DA:
            lo = _rot(lo, tda_ref, DA_ROT_DIM // 2)
            hi = _rot(hi, tda_ref, DA_ROT_DIM // 2)
            if c0 < COL_K_DA:
                lo = lo * (DA_QK_DIM ** -0.5 * LOG2E)
                hi = hi * (DA_QK_DIM ** -0.5 * LOG2E)
        elif c0 == COL_QK_R:
            lo = _rot(lo, tret_ref, RET_QK_DIM // 2)
            hi = _rot(hi, tret_ref, RET_QK_DIM // 2) * (RET_QK_DIM ** -0.5)
        o_ref[:, c0:c0 + LANES] = lo.astype(BF16)
        o_ref[:, c0 + LANES:c0 + MXU_COLS] = hi.astype(BF16)


def _in_proj(x, g, w, tda, tret, seq, layer):
    n = x.shape[0]
    tiles_per_seq = seq // TM_PROJ
    return pl.pallas_call(
        _in_proj_kernel,
        grid=(n // TM_PROJ,),
        in_specs=[
            pl.BlockSpec((TM_PROJ, D_MODEL), lambda i: (i, 0)),
            _layer_spec(g, layer),
            _layer_spec(w, layer),
            pl.BlockSpec((3, TM_PROJ, LANES), lambda i: (0, i % tiles_per_seq, 0)),
            pl.BlockSpec((3, TM_PROJ, LANES), lambda i: (0, i % tiles_per_seq, 0)),
        ],
        out_specs=pl.BlockSpec((TM_PROJ, IN_WIDTH), lambda i: (i, 0)),
        out_shape=jax.ShapeDtypeStruct((n, IN_WIDTH), BF16),
        compiler_params=_params("arbitrary"),
        name="in_proj",
    )(x, g, w, tda, tret)


def _diff_attn_kernel(q_ref, qn_ref, k_ref, v_ref, lq1_ref, lk1_ref, lq2_ref, lk2_ref, g_ref, o_ref,
                      vt_ref, qt_ref, qtn_ref, s0_ref, s1_ref, mx0_ref, mx1_ref, m_ref, acc_ref, *, lam_init):
    qi = pl.program_id(2)
    nkv = v_ref.shape[0] // TK
    s_refs, mx_refs = (s0_ref, s1_ref), (mx0_ref, mx1_ref)

    @pl.when(qi == 0)
    def _():
        for c in range(nkv):
            vt_ref[c, :DA_V_DIM, :] = v_ref[c * TK:(c + 1) * TK, :].astype(F32).T.astype(BF16)
            vt_ref[c, DA_V_DIM:, :] = jnp.ones((VT_ROWS - DA_V_DIM, TK), BF16)

    def store_qt(dst_ref, src_ref):
        qt = src_ref[...].astype(F32).T
        feat = lax.broadcasted_iota(jnp.int32, qt.shape, 0)
        dst_ref[0] = jnp.where(feat < DA_QK_DIM, qt, 0.0).astype(BF16)
        dst_ref[1] = jnp.where(feat >= DA_QK_DIM, qt, 0.0).astype(BF16)

    m_ref[...] = jnp.full(m_ref.shape, -jnp.inf, F32)
    acc_ref[...] = jnp.zeros(acc_ref.shape, F32)

    def columns(diag):
        out = []
        for qb in range(TQ // SUB):
            if diag is not None and diag * TK > (qb + 1) * SUB - 1:
                continue
            mask = diag is not None and (diag + 1) * TK - 1 > qb * SUB
            out += [(2 * qb + mi, qb, mi, mask) for mi in range(2)]
        return out

    def score(c, slot, diag, q_src=qt_ref):
        def one(j, qb, mi, mask):
            ks = k_ref[pl.ds(pl.multiple_of(c * TK, TK), TK), :]
            st = jnp.dot(ks, q_src[mi, :, qb * SUB:(qb + 1) * SUB],
                         preferred_element_type=F32)
            if mask:
                key = diag * TK + lax.broadcasted_iota(jnp.int32, st.shape, 0)
                qry = qb * SUB + lax.broadcasted_iota(jnp.int32, st.shape, 1)
                st = jnp.where(key <= qry, st, -jnp.inf)
            s_refs[slot][j] = st
            mx_refs[slot][j] = jnp.max(st, axis=0, keepdims=True)
        return [functools.partial(one, *col) for col in columns(diag)]

    def update(c, slot, diag):
        def one(j, qb, mi, mask):
            qs = slice(qb * SUB, (qb + 1) * SUB)
            m_prev = m_ref[mi, :, qs]
            m_new = jnp.maximum(m_prev, mx_refs[slot][j])
            alpha = jnp.exp2(m_prev - m_new)
            p = jnp.exp2(s_refs[slot][j] - m_new).astype(BF16)
            acc_ref[mi, :, qs] = (acc_ref[mi, :, qs] * alpha
                                  + jnp.dot(vt_ref[c], p, preferred_element_type=F32))
            m_ref[mi, :, qs] = m_new
        return [functools.partial(one, *col) for col in columns(diag)]

    def emit(scores, updates):
        scores = list(scores)
        for s in scores[:QK_AHEAD]:
            s()
        rest = scores[QK_AHEAD:]
        for u in updates:
            u()
            if rest:
                rest.pop(0)()
        for s in rest:
            s()

    assert TQ // TK == 2
    c0 = 2 * qi

    @pl.when(qi > 0)
    def _():
        qt_ref[...] = qtn_ref[...]

        def body(i, carry):
            emit(score(2 * i + 1, 1, None), update(2 * i, 0, None))
            emit(score(2 * i + 2, 0, None), update(2 * i + 1, 1, None))
            return carry

        lax.fori_loop(0, qi - 1, body, 0)
        emit(score(c0 - 1, 1, None), update(c0 - 2, 0, None))
        emit(score(c0, 0, 0), update(c0 - 1, 1, None))

    @pl.when(qi == 0)
    def _():
        store_qt(qt_ref, q_ref)
        emit(score(c0, 0, 0), [])

    store_qt(qtn_ref, qn_ref)
    emit(score(c0 + 1, 1, 1), update(c0, 0, 0))
    last_tile = qi == pl.num_programs(2) - 1

    @pl.when(jnp.logical_not(last_tile))
    def _():
        emit(score(0, 0, None, qtn_ref), update(c0 + 1, 1, 1))

    @pl.when(last_tile)
    def _():
        emit([], update(c0 + 1, 1, 1))

    lam = (jnp.exp(jnp.sum(lq1_ref[...] * lk1_ref[...], axis=-1, keepdims=True))
           - jnp.exp(jnp.sum(lq2_ref[...] * lk2_ref[...], axis=-1, keepdims=True)) + lam_init)
    o1 = acc_ref[0, :DA_V_DIM, :] / acc_ref[0, DA_V_DIM:DA_V_DIM + 1, :]
    o2 = acc_ref[1, :DA_V_DIM, :] / acc_ref[1, DA_V_DIM:DA_V_DIM + 1, :]
    a = (o1 - lam * o2).T
    o_ref[...] = (_rms(a, g_ref[...]) * (1.0 - lam_init)).astype(BF16)


def _diff_attn(proj, lq1, lk1, lq2, lk2, g, lam_init, batch, seq, layer):
    n = proj.shape[0]
    nq = seq // TQ
    kcol, vcol = COL_K_DA // LANES, COL_V_DA // LANES
    return pl.pallas_call(
        functools.partial(_diff_attn_kernel, lam_init=lam_init),
        grid=(batch, DA_HEADS, nq),
        in_specs=[
            pl.BlockSpec((TQ, LANES), lambda b, h, i: (b * nq + i, h)),
            pl.BlockSpec((TQ, LANES), lambda b, h, i: (b * nq + jnp.minimum(i + 1, nq - 1), h)),
            pl.BlockSpec((seq, LANES), lambda b, h, i: (b, kcol + h)),
            pl.BlockSpec((seq, LANES), lambda b, h, i: (b, vcol + h)),
            _layer_spec(lq1, layer), _layer_spec(lk1, layer), _layer_spec(lq2, layer), _layer_spec(lk2, layer),
            _layer_spec(g, layer),
        ],
        out_specs=pl.BlockSpec((TQ, LANES), lambda b, h, i: (b * nq + i, h)),
        out_shape=jax.ShapeDtypeStruct((n, DA_WIDTH), BF16),
        scratch_shapes=[
            pltpu.VMEM((seq // TK, VT_ROWS, TK), BF16),
            pltpu.VMEM((2, LANES, TQ), BF16),
            pltpu.VMEM((2, LANES, TQ), BF16),
            pltpu.VMEM((2 * TQ // SUB, TK, SUB), F32),
            pltpu.VMEM((2 * TQ // SUB, TK, SUB), F32),
            pltpu.VMEM((2 * TQ // SUB, 1, SUB), F32),
            pltpu.VMEM((2 * TQ // SUB, 1, SUB), F32),
            pltpu.VMEM((2, 1, TQ), F32),
            pltpu.VMEM((2, VT_ROWS, TQ), F32),
        ],
        compiler_params=_params("arbitrary", "arbitrary", "arbitrary"),
        name="diff_attn",
    )(proj, proj, proj, proj, lq1, lk1, lq2, lk2, g)


def _split_dot(x, w):
    hi = x.astype(BF16)
    lo = (x - hi.astype(F32)).astype(BF16)
    return (jnp.dot(hi, w, preferred_element_type=F32) + jnp.dot(lo, w, preferred_element_type=F32))


def _ret_pool_kernel(qk_ref, v_ref, g_ref, u_ref, decay_ref, xi_ref, zeta_ref, cd_ref, bd_ref,
                     avg_ref, retg_ref, win_ref, pw_ref, ps_ref, o_ref,
                     state_ref, halo_ref, oret_ref):
    i = pl.program_id(1)

    @pl.when(i == 0)
    def _():
        state_ref[...] = jnp.zeros(state_ref.shape, F32)
        halo_ref[...] = jnp.zeros(halo_ref.shape, F32)

    C = RET_CHUNK
    chunks = range(R_RET // C)
    qlane = lax.broadcasted_iota(jnp.int32, (C, LANES), 1) // RET_QK_DIM
    vlane = lax.broadcasted_iota(jnp.int32, (C, RET_WIDTH), 1) // RET_V_DIM
    qs = [qk_ref[c * C:(c + 1) * C, :LANES] for c in chunks]
    ks = [qk_ref[c * C:(c + 1) * C, LANES:] for c in chunks]
    vs = [v_ref[c * C:(c + 1) * C, :] for c in chunks]
    inner = [[lax.dot_general(jnp.where(qlane == h, qs[c], jnp.zeros_like(qs[c])), ks[c],
                              (((1,), (1,)), ((), ())), preferred_element_type=F32)
              for h in range(RET_HEADS)] for c in chunks]
    upd = [lax.dot_general((ks[c].astype(F32) * zeta_ref[...]).astype(BF16), vs[c],
                           (((0,), (0,)), ((), ())), preferred_element_type=F32) for c in chunks]
    states = [state_ref[...]]
    for c in chunks:
        states.append(states[c] * cd_ref[...] + bd_ref[...] * upd[c])
    state_ref[...] = states[-1]
    for c in chunks:
        o_c = jnp.dot((qs[c].astype(F32) * xi_ref[...]).astype(BF16), states[c].astype(BF16),
                      preferred_element_type=F32)
        for h in range(RET_HEADS):
            vh = jnp.where(vlane == h, vs[c], jnp.zeros_like(vs[c]))
            o_c = o_c + jnp.dot((inner[c][h] * decay_ref[h]).astype(BF16), vh, preferred_element_type=F32)
        oret_ref[c * C:(c + 1) * C, :] = o_c

    o = oret_ref[...]
    mu = _split_dot(o, avg_ref[...])
    d = o - mu
    var = jnp.dot((d * d).astype(BF16), avg_ref[...], preferred_element_type=F32)
    y = d * lax.rsqrt(var + EPS) * retg_ref[...]
    gate = g_ref[...].astype(F32)
    o_ref[:, :RET_WIDTH] = (gate * jax.nn.sigmoid(gate) * y).astype(BF16)

    u = u_ref[...].astype(F32)
    halo_groups = POOL_HALO // SUBLANES
    ext = jnp.concatenate([halo_ref[...], u], axis=0).reshape(-1, SUBLANES, POOL_WIDTH)
    halo_ref[...] = u[R_RET - POOL_HALO:, :]

    def shift_down(x, k):
        padded = jnp.concatenate([jnp.zeros_like(x[:1]), x], axis=0)
        if k == SUBLANES:
            return padded[:-1]
        r = pltpu.roll(padded, k, 1)
        return jnp.where(lax.broadcasted_iota(jnp.int32, x.shape, 1) < k, r[:-1], r[1:])

    assert POOL_WINDOWS == (2, 4, 8, 16) and POOL_DIM * 2 == LANES
    s2 = ext + shift_down(ext, 1)
    s4 = s2 + shift_down(s2, 2)
    s4_hi = s4[:, :, LANES:]
    s8 = s4_hi + shift_down(s4_hi, 4)
    s16 = s8 + shift_down(s8, 8)
    first = lax.broadcasted_iota(jnp.int32, s8.shape, 2) < POOL_DIM
    psum = jnp.concatenate([jnp.where(first, s2[:, :, :LANES], s4[:, :, :LANES]), jnp.where(first, s8, s16)],
                           axis=2)[halo_groups:].reshape(R_RET, POOL_WIDTH)
    t = (i * R_RET + lax.broadcasted_iota(jnp.int32, u.shape, 0)).astype(F32)
    pooled = psum / jnp.minimum(t + 1.0, win_ref[...]) - u
    yp = jnp.dot(pooled.astype(BF16), pw_ref[...], preferred_element_type=F32) * ps_ref[...]
    o_ref[:, RET_WIDTH:] = yp.astype(BF16)


def _ret_pool(proj, tabs, ret_g, pool_w_bd, pool_scale, batch, seq, layer):
    n = proj.shape[0]
    nr = seq // R_RET
    decay, xi, zeta, cd, bd, avg, win = tabs

    def col(c):
        return pl.BlockSpec((R_RET, MXU_COLS), lambda b, i: (b * nr + i, c // MXU_COLS))

    def const(shape):
        return pl.BlockSpec(shape, lambda b, i: (0,) * len(shape))

    return pl.pallas_call(
        _ret_pool_kernel,
        grid=(batch, nr),
        in_specs=[
            col(COL_QK_R), col(COL_V_R), col(COL_G_R), col(COL_U),
            const(decay.shape), const(xi.shape), const(zeta.shape), const(cd.shape), const(bd.shape),
            const(avg.shape), _layer_spec(ret_g, layer), const(win.shape), _layer_spec(pool_w_bd, layer),
            _layer_spec(pool_scale, layer),
        ],
        out_specs=pl.BlockSpec((R_RET, RET_WIDTH + POOL_WIDTH), lambda b, i: (b * nr + i, 0)),
        out_shape=jax.ShapeDtypeStruct((n, RET_WIDTH + POOL_WIDTH), BF16),
        scratch_shapes=[
            pltpu.VMEM((LANES, RET_WIDTH), F32),
            pltpu.VMEM((POOL_HALO, POOL_WIDTH), F32),
            pltpu.VMEM((R_RET, RET_WIDTH), F32),
        ],
        compiler_params=_params("arbitrary", "arbitrary"),
        name="ret_pool",
    )(proj, proj, proj, proj, decay, xi, zeta, cd, bd, avg, ret_g, win, pool_w_bd, pool_scale)


def _mix_up_kernel(oda_ref, orp_ref, x_ref, wo_ref, gpost_ref, gpre_ref, w_ref, cw_ref, cb_ref,
                   xo_ref, a_ref, carry_ref, ext_ref, *, tiles_per_seq):
    i = pl.program_id(0)
    tm = x_ref.shape[0]

    @pl.when(i % tiles_per_seq == 0)
    def _():
        carry_ref[...] = jnp.zeros(carry_ref.shape, F32)

    mix = (jnp.dot(oda_ref[...], wo_ref[:DA_WIDTH, :], preferred_element_type=F32)
           + jnp.dot(orp_ref[...], wo_ref[DA_WIDTH:, :], preferred_element_type=F32))
    xn = x_ref[...] + _rms(mix, gpost_ref[...])
    xo_ref[...] = xn
    h = _rms(xn, gpre_ref[...]).astype(BF16)

    def conv_cols(slot, c0):
        u = jnp.dot(h, w_ref[:, c0:c0 + MXU_COLS], preferred_element_type=F32)
        ext_ref[slot, :SUBLANES, :] = carry_ref[:, c0:c0 + MXU_COLS]
        ext_ref[slot, SUBLANES:, :] = u
        carry_ref[:, c0:c0 + MXU_COLS] = u[tm - SUBLANES:, :]
        cw = cw_ref[:, c0:c0 + MXU_COLS]
        return (cb_ref[:, c0:c0 + MXU_COLS]
                + cw[0:1] * ext_ref[slot, SUBLANES - 2:SUBLANES - 2 + tm, :]
                + cw[1:2] * ext_ref[slot, SUBLANES - 1:SUBLANES - 1 + tm, :]
                + cw[2:3] * u)

    for c in range(D_FF // MXU_COLS):
        c0 = c * MXU_COLS
        g = conv_cols(0, c0)
        v = conv_cols(1, D_FF + c0)
        th = jnp.tanh(g * (g * g * (GELU_C * 0.044715) + GELU_C))
        a_ref[:, c0:c0 + MXU_COLS] = (g * v * (th + 1.0)).astype(BF16)


def _mix_up(o_da, o_rp, x, w_out, g_post, g_pre, w_up, cw, cb, seq, layer):
    n = x.shape[0]
    row = lambda width: pl.BlockSpec((TM_PROJ, width), lambda i: (i, 0))

    def resident(arr):
        tail = arr.shape[1:]
        return pl.BlockSpec((None,) + tail, lambda *_: (layer,) + (0,) * len(tail),
                            pipeline_mode=pl.Buffered(1))

    return pl.pallas_call(
        functools.partial(_mix_up_kernel, tiles_per_seq=seq // TM_PROJ),
        grid=(n // TM_PROJ,),
        in_specs=[row(DA_WIDTH), row(RET_WIDTH + POOL_WIDTH), row(D_MODEL),
                  resident(w_out), _layer_spec(g_post, layer), _layer_spec(g_pre, layer),
                  resident(w_up), _layer_spec(cw, layer), _layer_spec(cb, layer)],
        out_specs=[row(D_MODEL), row(D_FF)],
        out_shape=[jax.ShapeDtypeStruct((n, D_MODEL), F32), jax.ShapeDtypeStruct((n, D_FF), BF16)],
        scratch_shapes=[
            pltpu.VMEM((SUBLANES, 2 * D_FF), F32),
            pltpu.VMEM((2, TM_PROJ + SUBLANES, MXU_COLS), F32),
        ],
        compiler_params=_params("arbitrary"),
        name="mix_up",
    )(o_da, o_rp, x, w_out, g_post, g_pre, w_up, cw, cb)


def _down_proj_kernel(a_ref, x_ref, w_ref, g_ref, xo_ref):
    y = jnp.dot(a_ref[...], w_ref[...], preferred_element_type=F32)
    xo_ref[...] = x_ref[...] + _rms(y, g_ref[...])


def _down_proj(a, x, w, g, layer):
    n = x.shape[0]
    row = lambda width: pl.BlockSpec((TM_PROJ, width), lambda i: (i, 0))
    return pl.pallas_call(
        _down_proj_kernel,
        grid=(n // TM_PROJ,),
        in_specs=[row(D_FF), row(D_MODEL), _layer_spec(w, layer), _layer_spec(g, layer)],
        out_specs=row(D_MODEL),
        out_shape=jax.ShapeDtypeStruct((n, D_MODEL), F32),
        compiler_params=_params("arbitrary"),
        name="down_proj",
    )(a, x, w, g)


def _rotary_tables(seq, rot_dim, theta, period):
    inv = jnp.float32(theta) ** (-jnp.arange(0, rot_dim, 2, dtype=F32) / rot_dim)
    ang = jnp.arange(seq, dtype=F32)[:, None] * inv[None, :]
    cos, sin = jnp.cos(ang), jnp.sin(ang)
    half = rot_dim // 2
    zh = jnp.zeros((seq, half), F32)
    zp = jnp.zeros((seq, period - rot_dim), F32)
    tabs = (jnp.concatenate([cos, cos, jnp.ones_like(zp)], axis=1),
            jnp.concatenate([zh, sin, zp], axis=1),
            jnp.concatenate([-sin, zh, zp], axis=1))
    return jnp.stack([jnp.tile(t, (1, LANES // period)) for t in tabs])


def _retention_tables():
    H, C = RET_HEADS, RET_CHUNK
    log_g = jnp.log(1.0 - 2.0 ** (-5.0 - jnp.arange(H, dtype=F32)))
    idx = jnp.arange(C, dtype=F32)
    diff = idx[:, None] - idx[None, :]
    decay = jnp.where(diff >= 0, jnp.exp(jnp.maximum(diff, 0.0) * log_g[:, None, None]), 0.0)
    xi = jnp.exp((idx + 1.0) * log_g[:, None])
    zeta = jnp.exp((C - 1.0 - idx) * log_g[:, None])
    chunk_decay = jnp.exp(C * log_g)
    xi_t = jnp.repeat(xi.T, RET_QK_DIM, axis=1)
    zeta_t = jnp.repeat(zeta.T, RET_QK_DIM, axis=1)
    row_head = jnp.arange(LANES) // RET_QK_DIM
    col_head = jnp.arange(RET_WIDTH) // RET_V_DIM
    bd = (row_head[:, None] == col_head[None, :]).astype(F32)
    cd = bd * chunk_decay[row_head][:, None]
    avg = ((col_head[:, None] == col_head[None, :]).astype(F32) / RET_V_DIM).astype(BF16)
    win = jnp.repeat(jnp.asarray(POOL_WINDOWS, F32), POOL_DIM)[None, :]
    return decay, xi_t, zeta_t, cd, bd, avg, win


def _block_diag(w):
    L, G, P, _ = w.shape
    eye = jnp.eye(G, dtype=w.dtype)
    return (eye[None, :, None, :, None] * w[:, :, :, None, :]).reshape(L, G * P, G * P)


def kernel(x, norm_mix_pre, norm_mix_post, w_in, lambda_q1, lambda_k1, lambda_q2, lambda_k2, diff_subln,
           ret_norm, pool_w, pool_scale, w_out, norm_mlp_pre, norm_mlp_post, w_up, conv_w, conv_b, w_down):
    batch, seq, _ = x.shape
    depth = w_in.shape[0]
    assert seq % TQ == 0 and seq % TM_PROJ == 0 and seq % R_RET == 0
    xf = x.reshape(batch * seq, D_MODEL)

    tda = _rotary_tables(seq, DA_ROT_DIM, ROPE_THETA, DA_QK_DIM)
    tret = _rotary_tables(seq, RET_QK_DIM, RET_THETA, RET_QK_DIM)
    rtabs = _retention_tables()

    rows = lambda a: a.reshape(depth, 1, -1)
    w_in_b, w_out_b, w_up_b, w_down_b = (w.astype(BF16) for w in (w_in, w_out, w_up, w_down))
    pool_w_b = _block_diag(pool_w).astype(BF16)
    half_val = jnp.concatenate([jnp.ones((D_FF,), F32), jnp.full((D_FF,), 0.5, F32)])
    conv_w_s, conv_b_s = conv_w * half_val, rows(conv_b * half_val)
    g_mix_pre, g_mix_post, g_mlp_pre, g_mlp_post = (rows(g) for g in (norm_mix_pre, norm_mix_post,
                                                                      norm_mlp_pre, norm_mlp_post))
    lq1, lk1, lq2, lk2 = (rows(v) for v in (lambda_q1, lambda_k1, lambda_q2, lambda_k2))
    g_sub, g_ret, p_scale = rows(diff_subln), rows(ret_norm), rows(pool_scale)

    for l in range(depth):
        lam_init = 0.8 - 0.6 * math.exp(-0.3 * l)
        proj = _in_proj(xf, g_mix_pre, w_in_b, tda, tret, seq, l)
        o_da = _diff_attn(proj, lq1, lk1, lq2, lk2, g_sub, lam_init, batch, seq, l)
        o_rp = _ret_pool(proj, rtabs, g_ret, pool_w_b, p_scale, batch, seq, l)
        xf, a = _mix_up(o_da, o_rp, xf, w_out_b, g_mix_post, g_mlp_pre, w_up_b, conv_w_s, conv_b_s, seq, l)
        xf = _down_proj(a, xf, w_down_b, g_mlp_post, l)
    return xf.reshape(batch, seq, D_MODEL)
```

```python
import functools
import math

import jax
import jax.numpy as jnp
import numpy as np
from jax import lax
from jax.experimental import pallas as pl
from jax.experimental.pallas import tpu as pltpu

F32 = jnp.float32
BF16 = jnp.bfloat16

D_MODEL = 1024
DA_HEADS = 4
DA_QK_DIM = 64
DA_V_DIM = 128
DA_ROT_DIM = 16
ROPE_THETA = 500000.0
RET_HEADS = 4
RET_QK_DIM = 32
RET_V_DIM = 64
RET_THETA = 10000.0
RET_CHUNK = 128
POOL_GROUPS = 4
POOL_DIM = 64
POOL_WINDOWS = (2, 4, 8, 16)
POOL_HALO = 16
DA_WIDTH = 512
RET_WIDTH = 256
POOL_WIDTH = 256
IN_WIDTH = 2560
D_FF = 2816
CONV_WIDTH = 3
EPS = 1e-6

COL_Q_DA, COL_K_DA, COL_V_DA = 0, 512, 1024
COL_QK_R, COL_V_R, COL_G_R, COL_U = 1536, 1792, 2048, 2304

LANES = 128
SUBLANES = 8
MXU_COLS = 256
VMEM_LIMIT = 56 * 1024 * 1024

TM_PROJ = 512
TQ = 1024
TK = 512
SUB = 256
QK_AHEAD = 2
VT_ROWS = DA_V_DIM + 16
LOG2E = math.log2(math.e)
GELU_C = math.sqrt(2.0 / math.pi)
R_RET = 512


def _params(*sem):
    return pltpu.CompilerParams(dimension_semantics=sem, vmem_limit_bytes=VMEM_LIMIT)


def _layer_spec(arr, layer):
    tail = arr.shape[1:]
    return pl.BlockSpec((None,) + tail, lambda *_: (layer,) + (0,) * len(tail))


def _rms(x, g):
    return x * lax.rsqrt(jnp.mean(x * x, axis=-1, keepdims=True) + EPS) * g


def _rot(x, tab_ref, shift):
    return (x * tab_ref[:, :LANES]
            + pltpu.roll(x, shift, 1) * tab_ref[:, LANES:2 * LANES]
            + pltpu.roll(x, LANES - shift, 1) * tab_ref[:, 2 * LANES:])


def _in_proj_kernel(x_ref, g_ref, w_ref, tda_ref, tret_ref, o_ref):
    h = _rms(x_ref[...], g_ref[...]).astype(BF16)
    for j in range(IN_WIDTH // MXU_COLS):
        c0 = j * MXU_COLS
        p = jnp.dot(h, w_ref[:, c0:c0 + MXU_COLS], preferred_element_type=F32)
        lo, hi = p[:, :LANES], p[:, LANES:]
        if c0 < COL_V_DA:
            lo = _rot(lo, tda_ref, DA_ROT_DIM // 2)
            hi = _rot(hi, tda_ref, DA_ROT_DIM // 2)
            if c0 < COL_K_DA:
                lo = lo * (DA_QK_DIM ** -0.5 * LOG2E)
                hi = hi * (DA_QK_DIM ** -0.5 * LOG2E)
        elif c0 == COL_QK_R:
            lo = _rot(lo, tret_ref, RET_QK_DIM // 2)
            hi = _rot(hi, tret_ref, RET_QK_DIM // 2) * (RET_QK_DIM ** -0.5)
        o_ref[:, c0:c0 + LANES] = lo.astype(BF16)
        o_ref[:, c0 + LANES:c0 + MXU_COLS] = hi.astype(BF16)


def _in_proj(x, g, w, tda, tret, seq, layer):
    n = x.shape[0]
    tiles_per_seq = seq // TM_PROJ
    return pl.pallas_call(
        _in_proj_kernel,
        grid=(n // TM_PROJ,),
        in_specs=[
            pl.BlockSpec((TM_PROJ, D_MODEL), lambda i: (i, 0)),
            _layer_spec(g, layer),
            _layer_spec(w, layer),
            pl.BlockSpec((TM_PROJ, 3 * LANES), lambda i: (i % tiles_per_seq, 0)),
            pl.BlockSpec((TM_PROJ, 3 * LANES), lambda i: (i % tiles_per_seq, 0)),
        ],
        out_specs=pl.BlockSpec((TM_PROJ, IN_WIDTH), lambda i: (i, 0)),
        out_shape=jax.ShapeDtypeStruct((n, IN_WIDTH), BF16),
        compiler_params=_params("arbitrary"),
        name="in_proj",
    )(x, g, w, tda, tret)


def _diff_attn_kernel(q_ref, qn_ref, k_ref, v_ref, lq1_ref, lk1_ref, lq2_ref, lk2_ref, g_ref, o_ref,
                      vt_ref, qt_ref, qtn_ref, s0_ref, s1_ref, mx0_ref, mx1_ref, m_ref, acc_ref, *, lam_init):
    qi = pl.program_id(2)
    nkv = v_ref.shape[0] // TK
    s_refs, mx_refs = (s0_ref, s1_ref), (mx0_ref, mx1_ref)

    @pl.when(qi == 0)
    def _():
        for c in range(nkv):
            vt_ref[c, :DA_V_DIM, :] = v_ref[c * TK:(c + 1) * TK, :].astype(F32).T.astype(BF16)
            vt_ref[c, DA_V_DIM:, :] = jnp.ones((VT_ROWS - DA_V_DIM, TK), BF16)

    def store_qt(dst_ref, src_ref):
        qt = src_ref[...].astype(F32).T
        feat = lax.broadcasted_iota(jnp.int32, qt.shape, 0)
        dst_ref[0] = jnp.where(feat < DA_QK_DIM, qt, 0.0).astype(BF16)
        dst_ref[1] = jnp.where(feat >= DA_QK_DIM, qt, 0.0).astype(BF16)

    m_ref[...] = jnp.full(m_ref.shape, -jnp.inf, F32)
    acc_ref[...] = jnp.zeros(acc_ref.shape, F32)

    def columns(diag):
        out = []
        for qb in range(TQ // SUB):
            if diag is not None and diag * TK > (qb + 1) * SUB - 1:
                continue
            mask = diag is not None and (diag + 1) * TK - 1 > qb * SUB
            out += [(2 * qb + mi, qb, mi, mask) for mi in range(2)]
        return out

    def score(c, slot, diag, q_src=qt_ref):
        def one(j, qb, mi, mask):
            ks = k_ref[pl.ds(pl.multiple_of(c * TK, TK), TK), :]
            st = jnp.dot(ks, q_src[mi, :, qb * SUB:(qb + 1) * SUB],
                         preferred_element_type=F32)
            if mask:
                key = diag * TK + lax.broadcasted_iota(jnp.int32, st.shape, 0)
                qry = qb * SUB + lax.broadcasted_iota(jnp.int32, st.shape, 1)
                st = jnp.where(key <= qry, st, -jnp.inf)
            s_refs[slot][j] = st
            mx_refs[slot][j] = jnp.max(st, axis=0, keepdims=True)
        return [functools.partial(one, *col) for col in columns(diag)]

    def update(c, slot, diag):
        def one(j, qb, mi, mask):
            qs = slice(qb * SUB, (qb + 1) * SUB)
            m_prev = m_ref[mi, :, qs]
            m_new = jnp.maximum(m_prev, mx_refs[slot][j])
            alpha = jnp.exp2(m_prev - m_new)
            p = jnp.exp2(s_refs[slot][j] - m_new).astype(BF16)
            acc_ref[mi, :, qs] = (acc_ref[mi, :, qs] * alpha
                                  + jnp.dot(vt_ref[c], p, preferred_element_type=F32))
            m_ref[mi, :, qs] = m_new
        return [functools.partial(one, *col) for col in columns(diag)]

    def emit(scores, updates):
        scores = list(scores)
        for s in scores[:QK_AHEAD]:
            s()
        rest = scores[QK_AHEAD:]
        for u in updates:
            u()
            if rest:
                rest.pop(0)()
        for s in rest:
            s()

    assert TQ // TK == 2
    c0 = 2 * qi

    @pl.when(qi > 0)
    def _():
        qt_ref[...] = qtn_ref[...]

        def body(i, carry):
            emit(score(2 * i + 1, 1, None), update(2 * i, 0, None))
            emit(score(2 * i + 2, 0, None), update(2 * i + 1, 1, None))
            return carry

        lax.fori_loop(0, qi - 1, body, 0)
        emit(score(c0 - 1, 1, None), update(c0 - 2, 0, None))
        emit(score(c0, 0, 0), update(c0 - 1, 1, None))

    @pl.when(qi == 0)
    def _():
        store_qt(qt_ref, q_ref)
        emit(score(c0, 0, 0), [])

    store_qt(qtn_ref, qn_ref)
    emit(score(c0 + 1, 1, 1), update(c0, 0, 0))
    last_tile = qi == pl.num_programs(2) - 1

    @pl.when(jnp.logical_not(last_tile))
    def _():
        emit(score(0, 0, None, qtn_ref), update(c0 + 1, 1, 1))

    @pl.when(last_tile)
    def _():
        emit([], update(c0 + 1, 1, 1))

    lam = (jnp.exp(jnp.sum(lq1_ref[...] * lk1_ref[...], axis=-1, keepdims=True))
           - jnp.exp(jnp.sum(lq2_ref[...] * lk2_ref[...], axis=-1, keepdims=True)) + lam_init)
    o1 = acc_ref[0, :DA_V_DIM, :] / acc_ref[0, DA_V_DIM:DA_V_DIM + 1, :]
    o2 = acc_ref[1, :DA_V_DIM, :] / acc_ref[1, DA_V_DIM:DA_V_DIM + 1, :]
    a = o1 - lam * o2
    y = a * lax.rsqrt(jnp.mean(a * a, axis=0, keepdims=True) + EPS) * g_ref[...]
    o_ref[...] = (y * (1.0 - lam_init)).astype(BF16)


def _diff_attn(proj, lq1, lk1, lq2, lk2, g, lam_init, batch, seq, layer):
    n = proj.shape[0]
    nq = seq // TQ
    kcol, vcol = COL_K_DA // LANES, COL_V_DA // LANES
    return pl.pallas_call(
        functools.partial(_diff_attn_kernel, lam_init=lam_init),
        grid=(batch, DA_HEADS, nq),
        in_specs=[
            pl.BlockSpec((TQ, LANES), lambda b, h, i: (b * nq + i, h)),
            pl.BlockSpec((TQ, LANES), lambda b, h, i: (b * nq + jnp.minimum(i + 1, nq - 1), h)),
            pl.BlockSpec((seq, LANES), lambda b, h, i: (b, kcol + h)),
            pl.BlockSpec((seq, LANES), lambda b, h, i: (b, vcol + h)),
            _layer_spec(lq1, layer), _layer_spec(lk1, layer), _layer_spec(lq2, layer), _layer_spec(lk2, layer),
            _layer_spec(g, layer),
        ],
        out_specs=pl.BlockSpec((LANES, TQ), lambda b, h, i: (h, b * nq + i)),
        out_shape=jax.ShapeDtypeStruct((DA_WIDTH, n), BF16),
        scratch_shapes=[
            pltpu.VMEM((seq // TK, VT_ROWS, TK), BF16),
            pltpu.VMEM((2, LANES, TQ), BF16),
            pltpu.VMEM((2, LANES, TQ), BF16),
            pltpu.VMEM((2 * TQ // SUB, TK, SUB), F32),
            pltpu.VMEM((2 * TQ // SUB, TK, SUB), F32),
            pltpu.VMEM((2 * TQ // SUB, 1, SUB), F32),
            pltpu.VMEM((2 * TQ // SUB, 1, SUB), F32),
            pltpu.VMEM((2, 1, TQ), F32),
            pltpu.VMEM((2, VT_ROWS, TQ), F32),
        ],
        compiler_params=_params("arbitrary", "arbitrary", "arbitrary"),
        name="diff_attn",
    )(proj, proj, proj, proj, lq1, lk1, lq2, lk2, g)


def _split_dot(x, w):
    hi = x.astype(BF16)
    lo = (x - hi.astype(F32)).astype(BF16)
    return (jnp.dot(hi, w, preferred_element_type=F32) + jnp.dot(lo, w, preferred_element_type=F32))


def _ret_pool_kernel(qk_ref, v_ref, g_ref, u_ref, decay_ref, xi_ref, zeta_ref, cd_ref, bd_ref,
                     avg_ref, retg_ref, win_ref, pw_ref, ps_ref, o_ref,
                     state_ref, halo_ref, oret_ref):
    i = pl.program_id(1)

    @pl.when(i == 0)
    def _():
        state_ref[...] = jnp.zeros(state_ref.shape, F32)
        halo_ref[...] = jnp.zeros(halo_ref.shape, F32)

    C = RET_CHUNK
    chunks = range(R_RET // C)
    qlane = lax.broadcasted_iota(jnp.int32, (C, LANES), 1) // RET_QK_DIM
    vlane = lax.broadcasted_iota(jnp.int32, (C, RET_WIDTH), 1) // RET_V_DIM
    qs = [qk_ref[c * C:(c + 1) * C, :LANES] for c in chunks]
    ks = [qk_ref[c * C:(c + 1) * C, LANES:] for c in chunks]
    vs = [v_ref[c * C:(c + 1) * C, :] for c in chunks]
    inner = [[lax.dot_general(jnp.where(qlane == h, qs[c], jnp.zeros_like(qs[c])), ks[c],
                              (((1,), (1,)), ((), ())), preferred_element_type=F32)
              for h in range(RET_HEADS)] for c in chunks]
    upd = [lax.dot_general((ks[c].astype(F32) * zeta_ref[...]).astype(BF16), vs[c],
                           (((0,), (0,)), ((), ())), preferred_element_type=F32) for c in chunks]
    states = [state_ref[...]]
    for c in chunks:
        states.append(states[c] * cd_ref[...] + bd_ref[...] * upd[c])
    state_ref[...] = states[-1]
    for c in chunks:
        o_c = jnp.dot((qs[c].astype(F32) * xi_ref[...]).astype(BF16), states[c].astype(BF16),
                      preferred_element_type=F32)
        for h in range(RET_HEADS):
            vh = jnp.where(vlane == h, vs[c], jnp.zeros_like(vs[c]))
            o_c = o_c + jnp.dot((inner[c][h] * decay_ref[h]).astype(BF16), vh, preferred_element_type=F32)
        oret_ref[c * C:(c + 1) * C, :] = o_c

    o = oret_ref[...]
    mu = _split_dot(o, avg_ref[...])
    d = o - mu
    var = jnp.dot((d * d).astype(BF16), avg_ref[...], preferred_element_type=F32)
    y = d * lax.rsqrt(var + EPS) * retg_ref[...]
    gate = g_ref[...].astype(F32)
    o_ref[:, :RET_WIDTH] = (gate * jax.nn.sigmoid(gate) * y).astype(BF16)

    u = u_ref[...].astype(F32)
    halo_groups = POOL_HALO // SUBLANES
    ext = jnp.concatenate([halo_ref[...], u], axis=0).reshape(-1, SUBLANES, POOL_WIDTH)
    halo_ref[...] = u[R_RET - POOL_HALO:, :]

    def shift_down(x, k):
        padded = jnp.concatenate([jnp.zeros_like(x[:1]), x], axis=0)
        if k == SUBLANES:
            return padded[:-1]
        r = pltpu.roll(padded, k, 1)
        return jnp.where(lax.broadcasted_iota(jnp.int32, x.shape, 1) < k, r[:-1], r[1:])

    assert POOL_WINDOWS == (2, 4, 8, 16) and POOL_DIM * 2 == LANES
    s2 = ext + shift_down(ext, 1)
    s4 = s2 + shift_down(s2, 2)
    s4_hi = s4[:, :, LANES:]
    s8 = s4_hi + shift_down(s4_hi, 4)
    s16 = s8 + shift_down(s8, 8)
    first = lax.broadcasted_iota(jnp.int32, s8.shape, 2) < POOL_DIM
    psum = jnp.concatenate([jnp.where(first, s2[:, :, :LANES], s4[:, :, :LANES]), jnp.where(first, s8, s16)],
                           axis=2)[halo_groups:].reshape(R_RET, POOL_WIDTH)
    t = (i * R_RET + lax.broadcasted_iota(jnp.int32, u.shape, 0)).astype(F32)
    pooled = psum / jnp.minimum(t + 1.0, win_ref[...]) - u
    yp = jnp.dot(pooled.astype(BF16), pw_ref[...], preferred_element_type=F32) * ps_ref[...]
    o_ref[:, RET_WIDTH:] = yp.astype(BF16)


def _ret_pool(proj, tabs, ret_g, pool_w_bd, pool_scale, batch, seq, layer):
    n = proj.shape[0]
    nr = seq // R_RET
    decay, xi, zeta, cd, bd, avg, win = tabs

    def col(c):
        return pl.BlockSpec((R_RET, MXU_COLS), lambda b, i: (b * nr + i, c // MXU_COLS))

    def const(shape):
        return pl.BlockSpec(shape, lambda b, i: (0,) * len(shape))

    return pl.pallas_call(
        _ret_pool_kernel,
        grid=(batch, nr),
        in_specs=[
            col(COL_QK_R), col(COL_V_R), col(COL_G_R), col(COL_U),
            const(decay.shape), const(xi.shape), const(zeta.shape), const(cd.shape), const(bd.shape),
            const(avg.shape), _layer_spec(ret_g, layer), const(win.shape), _layer_spec(pool_w_bd, layer),
            _layer_spec(pool_scale, layer),
        ],
        out_specs=pl.BlockSpec((R_RET, RET_WIDTH + POOL_WIDTH), lambda b, i: (b * nr + i, 0)),
        out_shape=jax.ShapeDtypeStruct((n, RET_WIDTH + POOL_WIDTH), BF16),
        scratch_shapes=[
            pltpu.VMEM((LANES, RET_WIDTH), F32),
            pltpu.VMEM((POOL_HALO, POOL_WIDTH), F32),
            pltpu.VMEM((R_RET, RET_WIDTH), F32),
        ],
        compiler_params=_params("arbitrary", "arbitrary"),
        name="ret_pool",
    )(proj, proj, proj, proj, decay, xi, zeta, cd, bd, avg, ret_g, win, pool_w_bd, pool_scale)


def _mix_up_kernel(oda_ref, orp_ref, x_ref, wo_ref, gpost_ref, gpre_ref, w_ref, cw_ref, cb_ref,
                   xo_ref, a_ref, carry_ref, ext_ref, *, tiles_per_seq):
    i = pl.program_id(0)
    tm = x_ref.shape[0]

    @pl.when(i % tiles_per_seq == 0)
    def _():
        carry_ref[...] = jnp.zeros(carry_ref.shape, F32)

    mix = (lax.dot_general(oda_ref[...], wo_ref[:DA_WIDTH, :], (((0,), (0,)), ((), ())),
                           preferred_element_type=F32)
           + jnp.dot(orp_ref[...], wo_ref[DA_WIDTH:, :], preferred_element_type=F32))
    xn = x_ref[...] + _rms(mix, gpost_ref[...])
    xo_ref[...] = xn
    h = _rms(xn, gpre_ref[...]).astype(BF16)

    def conv_cols(slot, c0):
        u = jnp.dot(h, w_ref[:, c0:c0 + MXU_COLS], preferred_element_type=F32)
        ext_ref[slot, :SUBLANES, :] = carry_ref[:, c0:c0 + MXU_COLS]
        ext_ref[slot, SUBLANES:, :] = u
        carry_ref[:, c0:c0 + MXU_COLS] = u[tm - SUBLANES:, :]
        cw = cw_ref[:, c0:c0 + MXU_COLS]
        return (cb_ref[:, c0:c0 + MXU_COLS]
                + cw[0:1] * ext_ref[slot, SUBLANES - 2:SUBLANES - 2 + tm, :]
                + cw[1:2] * ext_ref[slot, SUBLANES - 1:SUBLANES - 1 + tm, :]
                + cw[2:3] * u)

    for c in range(D_FF // MXU_COLS):
        c0 = c * MXU_COLS
        g = conv_cols(0, c0)
        v = conv_cols(1, D_FF + c0)
        th = jnp.tanh(g * (g * g * (GELU_C * 0.044715) + GELU_C))
        a_ref[:, c0:c0 + MXU_COLS] = (g * v * (th + 1.0)).astype(BF16)


def _mix_up(o_da, o_rp, x, w_out, g_post, g_pre, w_up, cw, cb, seq, layer):
    n = x.shape[0]
    row = lambda width: pl.BlockSpec((TM_PROJ, width), lambda i: (i, 0))

    def resident(arr):
        tail = arr.shape[1:]
        return pl.BlockSpec((None,) + tail, lambda *_: (layer,) + (0,) * len(tail),
                            pipeline_mode=pl.Buffered(1))

    return pl.pallas_call(
        functools.partial(_mix_up_kernel, tiles_per_seq=seq // TM_PROJ),
        grid=(n // TM_PROJ,),
        in_specs=[pl.BlockSpec((DA_WIDTH, TM_PROJ), lambda i: (0, i)), row(RET_WIDTH + POOL_WIDTH), row(D_MODEL),
                  resident(w_out), _layer_spec(g_post, layer), _layer_spec(g_pre, layer),
                  resident(w_up), _layer_spec(cw, layer), _layer_spec(cb, layer)],
        out_specs=[row(D_MODEL), row(D_FF)],
        out_shape=[jax.ShapeDtypeStruct((n, D_MODEL), F32), jax.ShapeDtypeStruct((n, D_FF), BF16)],
        scratch_shapes=[
            pltpu.VMEM((SUBLANES, 2 * D_FF), F32),
            pltpu.VMEM((2, TM_PROJ + SUBLANES, MXU_COLS), F32),
        ],
        compiler_params=_params("arbitrary"),
        name="mix_up",
    )(o_da, o_rp, x, w_out, g_post, g_pre, w_up, cw, cb)


def _down_proj_kernel(a_ref, x_ref, w_ref, g_ref, xo_ref):
    y = jnp.dot(a_ref[...], w_ref[...], preferred_element_type=F32)
    xo_ref[...] = x_ref[...] + _rms(y, g_ref[...])


def _down_proj(a, x, w, g, layer):
    n = x.shape[0]
    row = lambda width: pl.BlockSpec((TM_PROJ, width), lambda i: (i, 0))
    return pl.pallas_call(
        _down_proj_kernel,
        grid=(n // TM_PROJ,),
        in_specs=[row(D_FF), row(D_MODEL), _layer_spec(w, layer), _layer_spec(g, layer)],
        out_specs=row(D_MODEL),
        out_shape=jax.ShapeDtypeStruct((n, D_MODEL), F32),
        compiler_params=_params("arbitrary"),
        name="down_proj",
    )(a, x, w, g)


def _rotary_tables(seq, rot_dim, theta, period):
    inv = jnp.float32(theta) ** (-jnp.arange(0, rot_dim, 2, dtype=F32) / rot_dim)
    ang = jnp.arange(seq, dtype=F32)[:, None] * inv[None, :]
    half = rot_dim // 2
    basis = jnp.concatenate([jnp.cos(ang), jnp.sin(ang), jnp.ones((seq, 1), F32)], axis=1)
    sel = np.zeros((2 * half + 1, 3 * LANES), np.float32)
    for lane in range(LANES):
        d = lane % period
        if d >= rot_dim:
            sel[2 * half, lane] = 1.0
            continue
        sel[d % half, lane] = 1.0
        if d >= half:
            sel[half + d % half, LANES + lane] = 1.0
        else:
            sel[half + d % half, 2 * LANES + lane] = -1.0
    return jnp.dot(basis, jnp.asarray(sel), precision=lax.Precision.HIGHEST)


def _retention_tables():
    H, C = RET_HEADS, RET_CHUNK
    f32 = np.float32
    log_g = np.log(f32(1.0) - f32(2.0) ** (f32(-5.0) - np.arange(H, dtype=f32)))
    idx = np.arange(C, dtype=f32)
    diff = idx[:, None] - idx[None, :]
    decay = np.where(diff >= 0, np.exp(np.maximum(diff, f32(0.0)) * log_g[:, None, None]), f32(0.0))
    xi = np.exp((idx + f32(1.0)) * log_g[:, None])
    zeta = np.exp((f32(C) - f32(1.0) - idx) * log_g[:, None])
    chunk_decay = np.exp(f32(C) * log_g)
    xi_t = np.repeat(xi.T, RET_QK_DIM, axis=1)
    zeta_t = np.repeat(zeta.T, RET_QK_DIM, axis=1)
    row_head = np.arange(LANES) // RET_QK_DIM
    col_head = np.arange(RET_WIDTH) // RET_V_DIM
    bd = (row_head[:, None] == col_head[None, :]).astype(f32)
    cd = bd * chunk_decay[row_head][:, None]
    avg = (col_head[:, None] == col_head[None, :]).astype(f32) / f32(RET_V_DIM)
    win = np.repeat(np.asarray(POOL_WINDOWS, f32), POOL_DIM)[None, :]
    tabs = [jnp.asarray(t.astype(f32)) for t in (decay, xi_t, zeta_t, cd, bd)]
    return (*tabs, jnp.asarray(avg, BF16), jnp.asarray(win))


def _block_diag(w):
    L, G, P, _ = w.shape
    eye = jnp.eye(G, dtype=w.dtype)
    return (eye[None, :, None, :, None] * w[:, :, :, None, :]).reshape(L, G * P, G * P)


def kernel(x, norm_mix_pre, norm_mix_post, w_in, lambda_q1, lambda_k1, lambda_q2, lambda_k2, diff_subln,
           ret_norm, pool_w, pool_scale, w_out, norm_mlp_pre, norm_mlp_post, w_up, conv_w, conv_b, w_down):
    batch, seq, _ = x.shape
    depth = w_in.shape[0]
    assert seq % TQ == 0 and seq % TM_PROJ == 0 and seq % R_RET == 0
    xf = x.reshape(batch * seq, D_MODEL)

    tda = _rotary_tables(seq, DA_ROT_DIM, ROPE_THETA, DA_QK_DIM)
    tret = _rotary_tables(seq, RET_QK_DIM, RET_THETA, RET_QK_DIM)
    rtabs = _retention_tables()

    rows = lambda a: a.reshape(depth, 1, -1)
    w_in_b, w_out_b, w_up_b, w_down_b = (w.astype(BF16) for w in (w_in, w_out, w_up, w_down))
    pool_w_b = _block_diag(pool_w).astype(BF16)
    half_val = jnp.concatenate([jnp.ones((D_FF,), F32), jnp.full((D_FF,), 0.5, F32)])
    conv_w_s, conv_b_s = conv_w * half_val, rows(conv_b * half_val)
    g_mix_pre, g_mix_post, g_mlp_pre, g_mlp_post = (rows(g) for g in (norm_mix_pre, norm_mix_post,
                                                                      norm_mlp_pre, norm_mlp_post))
    lq1, lk1, lq2, lk2 = (rows(v) for v in (lambda_q1, lambda_k1, lambda_q2, lambda_k2))
    g_sub = diff_subln.reshape(depth, DA_V_DIM, 1)
    g_ret, p_scale = rows(ret_norm), rows(pool_scale)

    for l in range(depth):
        lam_init = 0.8 - 0.6 * math.exp(-0.3 * l)
        proj = _in_proj(xf, g_mix_pre, w_in_b, tda, tret, seq, l)
        o_da = _diff_attn(proj, lq1, lk1, lq2, lk2, g_sub, lam_init, batch, seq, l)
        o_rp = _ret_pool(proj, rtabs, g_ret, pool_w_b, p_scale, batch, seq, l)
        xf, a = _mix_up(o_da, o_rp, xf, w_out_b, g_mix_post, g_mlp_pre, w_up_b, conv_w_s, conv_b_s, seq, l)
        xf = _down_proj(a, xf, w_down_b, g_mlp_post, l)
    return xf.reshape(batch, seq, D_MODEL)
```

```python
import functools
import math

import jax
import jax.numpy as jnp
import numpy as np
from jax import lax
from jax.experimental import pallas as pl
from jax.experimental.pallas import tpu as pltpu

F32 = jnp.float32
BF16 = jnp.bfloat16

D_MODEL = 1024
DA_HEADS = 4
DA_QK_DIM = 64
DA_V_DIM = 128
DA_ROT_DIM = 16
ROPE_THETA = 500000.0
RET_HEADS = 4
RET_QK_DIM = 32
RET_V_DIM = 64
RET_THETA = 10000.0
RET_CHUNK = 128
POOL_GROUPS = 4
POOL_DIM = 64
POOL_WINDOWS = (2, 4, 8, 16)
POOL_HALO = 16
DA_WIDTH = 512
RET_WIDTH = 256
POOL_WIDTH = 256
IN_WIDTH = 2560
D_FF = 2816
CONV_WIDTH = 3
EPS = 1e-6

COL_Q_DA, COL_K_DA, COL_V_DA = 0, 512, 1024
COL_QK_R, COL_V_R, COL_G_R, COL_U = 1536, 1792, 2048, 2304

LANES = 128
SUBLANES = 8
MXU_COLS = 256
VMEM_LIMIT = 56 * 1024 * 1024

TM_PROJ = 512
TQ = 1024
TK = 512
SUB = 256
QK_AHEAD = 2
VT_ROWS = DA_V_DIM + 16
LOG2E = math.log2(math.e)
GELU_C = math.sqrt(2.0 / math.pi)
R_RET = 512


def _params(*sem):
    return pltpu.CompilerParams(dimension_semantics=sem, vmem_limit_bytes=VMEM_LIMIT)


def _layer_spec(arr, layer):
    tail = arr.shape[1:]
    return pl.BlockSpec((None,) + tail, lambda *_: (layer,) + (0,) * len(tail))


def _rms(x, g):
    return x * lax.rsqrt(jnp.mean(x * x, axis=-1, keepdims=True) + EPS) * g


def _rot(x, tab_ref, shift):
    return (x * tab_ref[:, :LANES]
            + pltpu.roll(x, shift, 1) * tab_ref[:, LANES:2 * LANES]
            + pltpu.roll(x, LANES - shift, 1) * tab_ref[:, 2 * LANES:])


def _in_proj_kernel(x_ref, g_ref, w_ref, tda_ref, tret_ref, o_ref):
    h = _rms(x_ref[...], g_ref[...]).astype(BF16)
    for j in range(IN_WIDTH // MXU_COLS):
        c0 = j * MXU_COLS
        p = jnp.dot(h, w_ref[:, c0:c0 + MXU_COLS], preferred_element_type=F32)
        lo, hi = p[:, :LANES], p[:, LANES:]
        if c0 < COL_V_DA:
            lo = _rot(lo, tda_ref, DA_ROT_DIM // 2)
            hi = _rot(hi, tda_ref, DA_ROT_DIM // 2)
            if c0 < COL_K_DA:
                lo = lo * (DA_QK_DIM ** -0.5 * LOG2E)
                hi = hi * (DA_QK_DIM ** -0.5 * LOG2E)
        elif c0 == COL_QK_R:
            lo = _rot(lo, tret_ref, RET_QK_DIM // 2)
            hi = _rot(hi, tret_ref, RET_QK_DIM // 2) * (RET_QK_DIM ** -0.5)
        o_ref[:, c0:c0 + LANES] = lo.astype(BF16)
        o_ref[:, c0 + LANES:c0 + MXU_COLS] = hi.astype(BF16)


def _in_proj(x, g, w, tda, tret, seq, layer):
    n = x.shape[0]
    tiles_per_seq = seq // TM_PROJ
    return pl.pallas_call(
        _in_proj_kernel,
        grid=(n // TM_PROJ,),
        in_specs=[
            pl.BlockSpec((TM_PROJ, D_MODEL), lambda i: (i, 0)),
            _layer_spec(g, layer),
            _layer_spec(w, layer),
            pl.BlockSpec((TM_PROJ, 3 * LANES), lambda i: (i % tiles_per_seq, 0)),
            pl.BlockSpec((TM_PROJ, 3 * LANES), lambda i: (i % tiles_per_seq, 0)),
        ],
        out_specs=pl.BlockSpec((TM_PROJ, IN_WIDTH), lambda i: (i, 0)),
        out_shape=jax.ShapeDtypeStruct((n, IN_WIDTH), BF16),
        compiler_params=_params("arbitrary"),
        name="in_proj",
    )(x, g, w, tda, tret)


def _diff_attn_kernel(q_ref, qn_ref, k_ref, v_ref, lq1_ref, lk1_ref, lq2_ref, lk2_ref, g_ref, o_ref,
                      vt_ref, qt_ref, qtn_ref, s0_ref, s1_ref, mx0_ref, mx1_ref, m_ref, acc_ref, *, lam_init):
    qi = pl.program_id(2)
    nkv = v_ref.shape[0] // TK
    s_refs, mx_refs = (s0_ref, s1_ref), (mx0_ref, mx1_ref)

    @pl.when(qi == 0)
    def _():
        for c in range(nkv):
            vt_ref[c, :DA_V_DIM, :] = v_ref[c * TK:(c + 1) * TK, :].astype(F32).T.astype(BF16)
            vt_ref[c, DA_V_DIM:, :] = jnp.ones((VT_ROWS - DA_V_DIM, TK), BF16)

    def store_qt(dst_ref, src_ref):
        qt = src_ref[...].astype(F32).T
        feat = lax.broadcasted_iota(jnp.int32, qt.shape, 0)
        dst_ref[0] = jnp.where(feat < DA_QK_DIM, qt, 0.0).astype(BF16)
        dst_ref[1] = jnp.where(feat >= DA_QK_DIM, qt, 0.0).astype(BF16)

    m_ref[...] = jnp.full(m_ref.shape, -jnp.inf, F32)
    acc_ref[...] = jnp.zeros(acc_ref.shape, F32)

    def columns(diag):
        out = []
        for qb in range(TQ // SUB):
            if diag is not None and diag * TK > (qb + 1) * SUB - 1:
                continue
            mask = diag is not None and (diag + 1) * TK - 1 > qb * SUB
            out += [(2 * qb + mi, qb, mi, mask) for mi in range(2)]
        return out

    def score(c, slot, diag, q_src=qt_ref):
        def one(j, qb, mi, mask):
            ks = k_ref[pl.ds(pl.multiple_of(c * TK, TK), TK), :]
            st = jnp.dot(ks, q_src[mi, :, qb * SUB:(qb + 1) * SUB],
                         preferred_element_type=F32)
            if mask:
                key = diag * TK + lax.broadcasted_iota(jnp.int32, st.shape, 0)
                qry = qb * SUB + lax.broadcasted_iota(jnp.int32, st.shape, 1)
                st = jnp.where(key <= qry, st, -jnp.inf)
            s_refs[slot][j] = st
            mx_refs[slot][j] = jnp.max(st, axis=0, keepdims=True)
        return [functools.partial(one, *col) for col in columns(diag)]

    def update(c, slot, diag):
        def one(j, qb, mi, mask):
            qs = slice(qb * SUB, (qb + 1) * SUB)
            m_prev = m_ref[mi, :, qs]
            m_new = jnp.maximum(m_prev, mx_refs[slot][j])
            alpha = jnp.exp2(m_prev - m_new)
            p = jnp.exp2(s_refs[slot][j] - m_new).astype(BF16)
            acc_ref[mi, :, qs] = (acc_ref[mi, :, qs] * alpha
                                  + jnp.dot(vt_ref[c], p, preferred_element_type=F32))
            m_ref[mi, :, qs] = m_new
        return [functools.partial(one, *col) for col in columns(diag)]

    def emit(scores, updates):
        scores = list(scores)
        for s in scores[:QK_AHEAD]:
            s()
        rest = scores[QK_AHEAD:]
        for u in updates:
            u()
            if rest:
                rest.pop(0)()
        for s in rest:
            s()

    assert TQ // TK == 2
    c0 = 2 * qi

    @pl.when(qi > 0)
    def _():
        qt_ref[...] = qtn_ref[...]

        def body(i, carry):
            emit(score(2 * i + 1, 1, None), update(2 * i, 0, None))
            emit(score(2 * i + 2, 0, None), update(2 * i + 1, 1, None))
            return carry

        lax.fori_loop(0, qi - 1, body, 0)
        emit(score(c0 - 1, 1, None), update(c0 - 2, 0, None))
        emit(score(c0, 0, 0), update(c0 - 1, 1, None))

    @pl.when(qi == 0)
    def _():
        store_qt(qt_ref, q_ref)
        emit(score(c0, 0, 0), [])

    store_qt(qtn_ref, qn_ref)
    emit(score(c0 + 1, 1, 1), update(c0, 0, 0))
    last_tile = qi == pl.num_programs(2) - 1

    @pl.when(jnp.logical_not(last_tile))
    def _():
        emit(score(0, 0, None, qtn_ref), update(c0 + 1, 1, 1))

    @pl.when(last_tile)
    def _():
        emit([], update(c0 + 1, 1, 1))

    lam = (jnp.exp(jnp.sum(lq1_ref[...] * lk1_ref[...], axis=-1, keepdims=True))
           - jnp.exp(jnp.sum(lq2_ref[...] * lk2_ref[...], axis=-1, keepdims=True)) + lam_init)
    o1 = acc_ref[0, :DA_V_DIM, :] / acc_ref[0, DA_V_DIM:DA_V_DIM + 1, :]
    o2 = acc_ref[1, :DA_V_DIM, :] / acc_ref[1, DA_V_DIM:DA_V_DIM + 1, :]
    a = o1 - lam * o2
    y = a * lax.rsqrt(jnp.mean(a * a, axis=0, keepdims=True) + EPS) * g_ref[...]
    o_ref[...] = (y * (1.0 - lam_init)).astype(BF16)


def _diff_attn(proj, lq1, lk1, lq2, lk2, g, lam_init, batch, seq, layer):
    n = proj.shape[0]
    nq = seq // TQ
    kcol, vcol = COL_K_DA // LANES, COL_V_DA // LANES
    return pl.pallas_call(
        functools.partial(_diff_attn_kernel, lam_init=lam_init),
        grid=(batch, DA_HEADS, nq),
        in_specs=[
            pl.BlockSpec((TQ, LANES), lambda b, h, i: (b * nq + i, h)),
            pl.BlockSpec((TQ, LANES), lambda b, h, i: (b * nq + jnp.minimum(i + 1, nq - 1), h)),
            pl.BlockSpec((seq, LANES), lambda b, h, i: (b, kcol + h)),
            pl.BlockSpec((seq, LANES), lambda b, h, i: (b, vcol + h)),
            _layer_spec(lq1, layer), _layer_spec(lk1, layer), _layer_spec(lq2, layer), _layer_spec(lk2, layer),
            _layer_spec(g, layer),
        ],
        out_specs=pl.BlockSpec((LANES, TQ), lambda b, h, i: (h, b * nq + i)),
        out_shape=jax.ShapeDtypeStruct((DA_WIDTH, n), BF16),
        scratch_shapes=[
            pltpu.VMEM((seq // TK, VT_ROWS, TK), BF16),
            pltpu.VMEM((2, LANES, TQ), BF16),
            pltpu.VMEM((2, LANES, TQ), BF16),
            pltpu.VMEM((2 * TQ // SUB, TK, SUB), F32),
            pltpu.VMEM((2 * TQ // SUB, TK, SUB), F32),
            pltpu.VMEM((2 * TQ // SUB, 1, SUB), F32),
            pltpu.VMEM((2 * TQ // SUB, 1, SUB), F32),
            pltpu.VMEM((2, 1, TQ), F32),
            pltpu.VMEM((2, VT_ROWS, TQ), F32),
        ],
        compiler_params=_params("arbitrary", "arbitrary", "arbitrary"),
        name="diff_attn",
    )(proj, proj, proj, proj, lq1, lk1, lq2, lk2, g)


def _split_dot(x, w):
    hi = x.astype(BF16)
    lo = (x - hi.astype(F32)).astype(BF16)
    return (jnp.dot(hi, w, preferred_element_type=F32) + jnp.dot(lo, w, preferred_element_type=F32))


def _ret_pool_kernel(qk_ref, v_ref, g_ref, u_ref, decay_ref, xi_ref, zeta_ref, cd_ref, bd_ref,
                     avg_ref, retg_ref, win_ref, pw_ref, ps_ref, o_ref,
                     state_ref, halo_ref, oret_ref):
    i = pl.program_id(1)

    @pl.when(i == 0)
    def _():
        state_ref[...] = jnp.zeros(state_ref.shape, F32)
        halo_ref[...] = jnp.zeros(halo_ref.shape, F32)

    C = RET_CHUNK
    chunks = range(R_RET // C)
    qlane = lax.broadcasted_iota(jnp.int32, (C, LANES), 1) // RET_QK_DIM
    vlane = lax.broadcasted_iota(jnp.int32, (C, RET_WIDTH), 1) // RET_V_DIM
    qs = [qk_ref[c * C:(c + 1) * C, :LANES] for c in chunks]
    ks = [qk_ref[c * C:(c + 1) * C, LANES:] for c in chunks]
    vs = [v_ref[c * C:(c + 1) * C, :] for c in chunks]
    inner = [[lax.dot_general(jnp.where(qlane == h, qs[c], jnp.zeros_like(qs[c])), ks[c],
                              (((1,), (1,)), ((), ())), preferred_element_type=F32)
              for h in range(RET_HEADS)] for c in chunks]
    upd = [lax.dot_general((ks[c].astype(F32) * zeta_ref[...]).astype(BF16), vs[c],
                           (((0,), (0,)), ((), ())), preferred_element_type=F32) for c in chunks]
    states = [state_ref[...]]
    for c in chunks:
        states.append(states[c] * cd_ref[...] + bd_ref[...] * upd[c])
    state_ref[...] = states[-1]
    for c in chunks:
        o_c = jnp.dot((qs[c].astype(F32) * xi_ref[...]).astype(BF16), states[c].astype(BF16),
                      preferred_element_type=F32)
        for h in range(RET_HEADS):
            vh = jnp.where(vlane == h, vs[c], jnp.zeros_like(vs[c]))
            o_c = o_c + jnp.dot((inner[c][h] * decay_ref[h]).astype(BF16), vh, preferred_element_type=F32)
        oret_ref[c * C:(c + 1) * C, :] = o_c

    o = oret_ref[...]
    mu = _split_dot(o, avg_ref[...])
    d = o - mu
    var = jnp.dot((d * d).astype(BF16), avg_ref[...], preferred_element_type=F32)
    y = d * lax.rsqrt(var + EPS) * retg_ref[...]
    gate = g_ref[...].astype(F32)
    o_ref[:, :RET_WIDTH] = (gate * jax.nn.sigmoid(gate) * y).astype(BF16)

    u = u_ref[...].astype(F32)
    halo_groups = POOL_HALO // SUBLANES
    ext = jnp.concatenate([halo_ref[...], u], axis=0).reshape(-1, SUBLANES, POOL_WIDTH)
    halo_ref[...] = u[R_RET - POOL_HALO:, :]

    def shift_down(x, k):
        padded = jnp.concatenate([jnp.zeros_like(x[:1]), x], axis=0)
        if k == SUBLANES:
            return padded[:-1]
        r = pltpu.roll(padded, k, 1)
        return jnp.where(lax.broadcasted_iota(jnp.int32, x.shape, 1) < k, r[:-1], r[1:])

    assert POOL_WINDOWS == (2, 4, 8, 16) and POOL_DIM * 2 == LANES
    s2 = ext + shift_down(ext, 1)
    s4 = s2 + shift_down(s2, 2)
    s4_hi = s4[:, :, LANES:]
    s8 = s4_hi + shift_down(s4_hi, 4)
    s16 = s8 + shift_down(s8, 8)
    first = lax.broadcasted_iota(jnp.int32, s8.shape, 2) < POOL_DIM
    psum = jnp.concatenate([jnp.where(first, s2[:, :, :LANES], s4[:, :, :LANES]), jnp.where(first, s8, s16)],
                           axis=2)[halo_groups:].reshape(R_RET, POOL_WIDTH)
    t = (i * R_RET + lax.broadcasted_iota(jnp.int32, u.shape, 0)).astype(F32)
    pooled = psum / jnp.minimum(t + 1.0, win_ref[...]) - u
    yp = jnp.dot(pooled.astype(BF16), pw_ref[...], preferred_element_type=F32) * ps_ref[...]
    o_ref[:, RET_WIDTH:] = yp.astype(BF16)


def _ret_pool(proj, tabs, ret_g, pool_w_bd, pool_scale, batch, seq, layer):
    n = proj.shape[0]
    nr = seq // R_RET
    decay, xi, zeta, cd, bd, avg, win = tabs

    def col(c):
        return pl.BlockSpec((R_RET, MXU_COLS), lambda b, i: (b * nr + i, c // MXU_COLS))

    def const(shape):
        return pl.BlockSpec(shape, lambda b, i: (0,) * len(shape))

    return pl.pallas_call(
        _ret_pool_kernel,
        grid=(batch, nr),
        in_specs=[
            col(COL_QK_R), col(COL_V_R), col(COL_G_R), col(COL_U),
            const(decay.shape), const(xi.shape), const(zeta.shape), const(cd.shape), const(bd.shape),
            const(avg.shape), _layer_spec(ret_g, layer), const(win.shape), _layer_spec(pool_w_bd, layer),
            _layer_spec(pool_scale, layer),
        ],
        out_specs=pl.BlockSpec((R_RET, RET_WIDTH + POOL_WIDTH), lambda b, i: (b * nr + i, 0)),
        out_shape=jax.ShapeDtypeStruct((n, RET_WIDTH + POOL_WIDTH), BF16),
        scratch_shapes=[
            pltpu.VMEM((LANES, RET_WIDTH), F32),
            pltpu.VMEM((POOL_HALO, POOL_WIDTH), F32),
            pltpu.VMEM((R_RET, RET_WIDTH), F32),
        ],
        compiler_params=_params("arbitrary", "arbitrary"),
        name="ret_pool",
    )(proj, proj, proj, proj, decay, xi, zeta, cd, bd, avg, ret_g, win, pool_w_bd, pool_scale)


def _mix_up_kernel(oda_ref, orp_ref, x_ref, wo_ref, gpost_ref, gpre_ref, w_ref, cw_ref, cb_ref,
                   xo_ref, a_ref, carry_ref, ext_ref, *, tiles_per_seq):
    i = pl.program_id(0)
    tm = x_ref.shape[0]

    @pl.when(i % tiles_per_seq == 0)
    def _():
        carry_ref[...] = jnp.zeros(carry_ref.shape, F32)

    mix = (lax.dot_general(oda_ref[...], wo_ref[:DA_WIDTH, :], (((0,), (0,)), ((), ())),
                           preferred_element_type=F32)
           + jnp.dot(orp_ref[...], wo_ref[DA_WIDTH:, :], preferred_element_type=F32))
    xn = x_ref[...] + _rms(mix, gpost_ref[...])
    xo_ref[...] = xn
    h = _rms(xn, gpre_ref[...]).astype(BF16)

    def conv_cols(slot, c0):
        u = jnp.dot(h, w_ref[:, c0:c0 + MXU_COLS], preferred_element_type=F32)
        ext_ref[slot, :SUBLANES, :] = carry_ref[:, c0:c0 + MXU_COLS]
        ext_ref[slot, SUBLANES:, :] = u
        carry_ref[:, c0:c0 + MXU_COLS] = u[tm - SUBLANES:, :]
        cw = cw_ref[:, c0:c0 + MXU_COLS]
        return (cb_ref[:, c0:c0 + MXU_COLS]
                + cw[0:1] * ext_ref[slot, SUBLANES - 2:SUBLANES - 2 + tm, :]
                + cw[1:2] * ext_ref[slot, SUBLANES - 1:SUBLANES - 1 + tm, :]
                + cw[2:3] * u)

    for c in range(D_FF // MXU_COLS):
        c0 = c * MXU_COLS
        g = conv_cols(0, c0)
        v = conv_cols(1, D_FF + c0)
        th = jnp.tanh(g * (g * g * (GELU_C * 0.044715) + GELU_C))
        a_ref[:, c0:c0 + MXU_COLS] = (g * v * (th + 1.0)).astype(BF16)


def _mix_up(o_da, o_rp, x, w_out, g_post, g_pre, w_up, cw, cb, seq, layer):
    n = x.shape[0]
    row = lambda width: pl.BlockSpec((TM_PROJ, width), lambda i: (i, 0))

    def resident(arr):
        tail = arr.shape[1:]
        return pl.BlockSpec((None,) + tail, lambda *_: (layer,) + (0,) * len(tail),
                            pipeline_mode=pl.Buffered(1))

    return pl.pallas_call(
        functools.partial(_mix_up_kernel, tiles_per_seq=seq // TM_PROJ),
        grid=(n // TM_PROJ,),
        in_specs=[pl.BlockSpec((DA_WIDTH, TM_PROJ), lambda i: (0, i)), row(RET_WIDTH + POOL_WIDTH), row(D_MODEL),
                  resident(w_out), _layer_spec(g_post, layer), _layer_spec(g_pre, layer),
                  resident(w_up), _layer_spec(cw, layer), _layer_spec(cb, layer)],
        out_specs=[row(D_MODEL), row(D_FF)],
        out_shape=[jax.ShapeDtypeStruct((n, D_MODEL), F32), jax.ShapeDtypeStruct((n, D_FF), BF16)],
        scratch_shapes=[
            pltpu.VMEM((SUBLANES, 2 * D_FF), F32),
            pltpu.VMEM((2, TM_PROJ + SUBLANES, MXU_COLS), F32),
        ],
        compiler_params=_params("arbitrary"),
        name="mix_up",
    )(o_da, o_rp, x, w_out, g_post, g_pre, w_up, cw, cb)


def _down_proj_kernel(a_ref, x_ref, w_ref, g_ref, xo_ref):
    y = jnp.dot(a_ref[...], w_ref[...], preferred_element_type=F32)
    xo_ref[...] = x_ref[...] + _rms(y, g_ref[...])


def _down_proj(a, x, w, g, layer):
    n = x.shape[0]
    row = lambda width: pl.BlockSpec((TM_PROJ, width), lambda i: (i, 0))
    return pl.pallas_call(
        _down_proj_kernel,
        grid=(n // TM_PROJ,),
        in_specs=[row(D_FF), row(D_MODEL), _layer_spec(w, layer), _layer_spec(g, layer)],
        out_specs=row(D_MODEL),
        out_shape=jax.ShapeDtypeStruct((n, D_MODEL), F32),
        compiler_params=_params("arbitrary"),
        name="down_proj",
    )(a, x, w, g)


def _rotary_tables(seq, rot_dim, theta, period):
    inv = jnp.float32(theta) ** (-jnp.arange(0, rot_dim, 2, dtype=F32) / rot_dim)
    ang = jnp.arange(seq, dtype=F32)[:, None] * inv[None, :]
    cos, sin = jnp.cos(ang), jnp.sin(ang)
    half = rot_dim // 2
    zh = jnp.zeros((seq, half), F32)
    zp = jnp.zeros((seq, period - rot_dim), F32)
    one_period = jnp.stack([jnp.concatenate([cos, cos, jnp.ones_like(zp)], axis=1),
                            jnp.concatenate([zh, sin, zp], axis=1),
                            jnp.concatenate([-sin, zh, zp], axis=1)], axis=1)
    return jnp.broadcast_to(one_period[:, :, None, :],
                            (seq, 3, LANES // period, period)).reshape(seq, 3 * LANES)


def _retention_tables():
    H, C = RET_HEADS, RET_CHUNK
    f32 = np.float32
    log_g = np.log(f32(1.0) - f32(2.0) ** (f32(-5.0) - np.arange(H, dtype=f32)))
    idx = np.arange(C, dtype=f32)
    diff = idx[:, None] - idx[None, :]
    decay = np.where(diff >= 0, np.exp(np.maximum(diff, f32(0.0)) * log_g[:, None, None]), f32(0.0))
    xi = np.exp((idx + f32(1.0)) * log_g[:, None])
    zeta = np.exp((f32(C) - f32(1.0) - idx) * log_g[:, None])
    chunk_decay = np.exp(f32(C) * log_g)
    xi_t = np.repeat(xi.T, RET_QK_DIM, axis=1)
    zeta_t = np.repeat(zeta.T, RET_QK_DIM, axis=1)
    row_head = np.arange(LANES) // RET_QK_DIM
    col_head = np.arange(RET_WIDTH) // RET_V_DIM
    bd = (row_head[:, None] == col_head[None, :]).astype(f32)
    cd = bd * chunk_decay[row_head][:, None]
    avg = (col_head[:, None] == col_head[None, :]).astype(f32) / f32(RET_V_DIM)
    win = np.repeat(np.asarray(POOL_WINDOWS, f32), POOL_DIM)[None, :]
    tabs = [jnp.asarray(t.astype(f32)) for t in (decay, xi_t, zeta_t, cd, bd)]
    return (*tabs, jnp.asarray(avg, BF16), jnp.asarray(win))


def _block_diag(w):
    L, G, P, _ = w.shape
    eye = jnp.eye(G, dtype=w.dtype)
    return (eye[None, :, None, :, None] * w[:, :, :, None, :]).reshape(L, G * P, G * P)


def kernel(x, norm_mix_pre, norm_mix_post, w_in, lambda_q1, lambda_k1, lambda_q2, lambda_k2, diff_subln,
           ret_norm, pool_w, pool_scale, w_out, norm_mlp_pre, norm_mlp_post, w_up, conv_w, conv_b, w_down):
    batch, seq, _ = x.shape
    depth = w_in.shape[0]
    assert seq % TQ == 0 and seq % TM_PROJ == 0 and seq % R_RET == 0
    xf = x.reshape(batch * seq, D_MODEL)

    tda = _rotary_tables(seq, DA_ROT_DIM, ROPE_THETA, DA_QK_DIM)
    tret = _rotary_tables(seq, RET_QK_DIM, RET_THETA, RET_QK_DIM)
    rtabs = _retention_tables()

    rows = lambda a: a.reshape(depth, 1, -1)
    w_in_b, w_out_b, w_up_b, w_down_b = (w.astype(BF16) for w in (w_in, w_out, w_up, w_down))
    pool_w_b = _block_diag(pool_w).astype(BF16)
    half_val = jnp.concatenate([jnp.ones((D_FF,), F32), jnp.full((D_FF,), 0.5, F32)])
    conv_w_s, conv_b_s = conv_w * half_val, rows(conv_b * half_val)
    g_mix_pre, g_mix_post, g_mlp_pre, g_mlp_post = (rows(g) for g in (norm_mix_pre, norm_mix_post,
                                                                      norm_mlp_pre, norm_mlp_post))
    lq1, lk1, lq2, lk2 = (rows(v) for v in (lambda_q1, lambda_k1, lambda_q2, lambda_k2))
    g_sub = diff_subln.reshape(depth, DA_V_DIM, 1)
    g_ret, p_scale = rows(ret_norm), rows(pool_scale)

    for l in range(depth):
        lam_init = 0.8 - 0.6 * math.exp(-0.3 * l)
        proj = _in_proj(xf, g_mix_pre, w_in_b, tda, tret, seq, l)
        o_da = _diff_attn(proj, lq1, lk1, lq2, lk2, g_sub, lam_init, batch, seq, l)
        o_rp = _ret_pool(proj, rtabs, g_ret, pool_w_b, p_scale, batch, seq, l)
        xf, a = _mix_up(o_da, o_rp, xf, w_out_b, g_mix_post, g_mlp_pre, w_up_b, conv_w_s, conv_b_s, seq, l)
        xf = _down_proj(a, xf, w_down_b, g_mlp_post, l)
    return xf.reshape(batch, seq, D_MODEL)
```

```python
import functools
import math

import jax
import jax.numpy as jnp
import numpy as np
from jax import lax
from jax.experimental import pallas as pl
from jax.experimental.pallas import tpu as pltpu

F32 = jnp.float32
BF16 = jnp.bfloat16

D_MODEL = 1024
DA_HEADS = 4
DA_QK_DIM = 64
DA_V_DIM = 128
DA_ROT_DIM = 16
ROPE_THETA = 500000.0
RET_HEADS = 4
RET_QK_DIM = 32
RET_V_DIM = 64
RET_THETA = 10000.0
RET_CHUNK = 128
POOL_GROUPS = 4
POOL_DIM = 64
POOL_WINDOWS = (2, 4, 8, 16)
POOL_HALO = 16
DA_WIDTH = 512
RET_WIDTH = 256
POOL_WIDTH = 256
IN_WIDTH = 2560
D_FF = 2816
CONV_WIDTH = 3
EPS = 1e-6

COL_Q_DA, COL_K_DA, COL_V_DA = 0, 512, 1024
COL_QK_R, COL_V_R, COL_G_R, COL_U = 1536, 1792, 2048, 2304

LANES = 128
SUBLANES = 8
MXU_COLS = 256
VMEM_LIMIT = 56 * 1024 * 1024

TM_PROJ = 1024
TM_MIX = 512
TQ = 1024
TK = 512
SUB = 256
QK_AHEAD = 2
VT_ROWS = DA_V_DIM + 16
LOG2E = math.log2(math.e)
GELU_C = math.sqrt(2.0 / math.pi)
R_RET = 512


def _params(*sem):
    return pltpu.CompilerParams(dimension_semantics=sem, vmem_limit_bytes=VMEM_LIMIT)


def _layer_spec(arr, layer):
    tail = arr.shape[1:]
    return pl.BlockSpec((None,) + tail, lambda *_: (layer,) + (0,) * len(tail))


def _rms(x, g):
    return x * lax.rsqrt(jnp.mean(x * x, axis=-1, keepdims=True) + EPS) * g


def _rot_tables(tab_ref, period):
    return [jnp.concatenate([tab_ref[:, t * period:(t + 1) * period]] * (LANES // period), axis=1)
            for t in range(3)]


def _rot(x, tabs, shift):
    return x * tabs[0] + pltpu.roll(x, shift, 1) * tabs[1] + pltpu.roll(x, LANES - shift, 1) * tabs[2]


def _in_proj_kernel(x_ref, g_ref, w_ref, tda_ref, tret_ref, o_ref):
    h = _rms(x_ref[...], g_ref[...]).astype(BF16)
    tda, tret = _rot_tables(tda_ref, DA_QK_DIM), _rot_tables(tret_ref, RET_QK_DIM)
    for j in range(IN_WIDTH // MXU_COLS):
        c0 = j * MXU_COLS
        p = jnp.dot(h, w_ref[:, c0:c0 + MXU_COLS], preferred_element_type=F32)
        lo, hi = p[:, :LANES], p[:, LANES:]
        if c0 < COL_V_DA:
            lo = _rot(lo, tda, DA_ROT_DIM // 2)
            hi = _rot(hi, tda, DA_ROT_DIM // 2)
            if c0 < COL_K_DA:
                lo = lo * (DA_QK_DIM ** -0.5 * LOG2E)
                hi = hi * (DA_QK_DIM ** -0.5 * LOG2E)
        elif c0 == COL_QK_R:
            lo = _rot(lo, tret, RET_QK_DIM // 2)
            hi = _rot(hi, tret, RET_QK_DIM // 2) * (RET_QK_DIM ** -0.5)
        o_ref[:, c0:c0 + LANES] = lo.astype(BF16)
        o_ref[:, c0 + LANES:c0 + MXU_COLS] = hi.astype(BF16)


def _in_proj(x, g, w, tda, tret, seq, layer):
    n = x.shape[0]
    tiles_per_seq = seq // TM_PROJ
    return pl.pallas_call(
        _in_proj_kernel,
        grid=(n // TM_PROJ,),
        in_specs=[
            pl.BlockSpec((TM_PROJ, D_MODEL), lambda i: (i, 0)),
            _layer_spec(g, layer),
            _layer_spec(w, layer),
            pl.BlockSpec((TM_PROJ, tda.shape[1]), lambda i: (i % tiles_per_seq, 0)),
            pl.BlockSpec((TM_PROJ, tret.shape[1]), lambda i: (i % tiles_per_seq, 0)),
        ],
        out_specs=pl.BlockSpec((TM_PROJ, IN_WIDTH), lambda i: (i, 0)),
        out_shape=jax.ShapeDtypeStruct((n, IN_WIDTH), BF16),
        compiler_params=_params("arbitrary"),
        name="in_proj",
    )(x, g, w, tda, tret)


def _diff_attn_kernel(q_ref, qn_ref, k_ref, v_ref, lq1_ref, lk1_ref, lq2_ref, lk2_ref, g_ref, o_ref,
                      vt_ref, qt_ref, qtn_ref, s0_ref, s1_ref, mx0_ref, mx1_ref, m_ref, acc_ref, *, lam_init):
    qi = pl.program_id(2)
    nkv = v_ref.shape[0] // TK
    s_refs, mx_refs = (s0_ref, s1_ref), (mx0_ref, mx1_ref)

    @pl.when(qi == 0)
    def _():
        for c in range(nkv):
            vt_ref[c, :DA_V_DIM, :] = v_ref[c * TK:(c + 1) * TK, :].astype(F32).T.astype(BF16)
            vt_ref[c, DA_V_DIM:, :] = jnp.ones((VT_ROWS - DA_V_DIM, TK), BF16)

    def store_qt(dst_ref, src_ref):
        qt = src_ref[...].astype(F32).T
        feat = lax.broadcasted_iota(jnp.int32, qt.shape, 0)
        dst_ref[0] = jnp.where(feat < DA_QK_DIM, qt, 0.0).astype(BF16)
        dst_ref[1] = jnp.where(feat >= DA_QK_DIM, qt, 0.0).astype(BF16)

    m_ref[...] = jnp.full(m_ref.shape, -jnp.inf, F32)
    acc_ref[...] = jnp.zeros(acc_ref.shape, F32)

    def columns(diag):
        out = []
        for qb in range(TQ // SUB):
            if diag is not None and diag * TK > (qb + 1) * SUB - 1:
                continue
            mask = diag is not None and (diag + 1) * TK - 1 > qb * SUB
            out += [(2 * qb + mi, qb, mi, mask) for mi in range(2)]
        return out

    def score(c, slot, diag, q_src=qt_ref):
        def one(j, qb, mi, mask):
            ks = k_ref[pl.ds(pl.multiple_of(c * TK, TK), TK), :]
            st = jnp.dot(ks, q_src[mi, :, qb * SUB:(qb + 1) * SUB],
                         preferred_element_type=F32)
            if mask:
                key = diag * TK + lax.broadcasted_iota(jnp.int32, st.shape, 0)
                qry = qb * SUB + lax.broadcasted_iota(jnp.int32, st.shape, 1)
                st = jnp.where(key <= qry, st, -jnp.inf)
            s_refs[slot][j] = st
            mx_refs[slot][j] = jnp.max(st, axis=0, keepdims=True)
        return [functools.partial(one, *col) for col in columns(diag)]

    def update(c, slot, diag):
        def one(j, qb, mi, mask):
            qs = slice(qb * SUB, (qb + 1) * SUB)
            m_prev = m_ref[mi, :, qs]
            m_new = jnp.maximum(m_prev, mx_refs[slot][j])
            alpha = jnp.exp2(m_prev - m_new)
            p = jnp.exp2(s_refs[slot][j] - m_new).astype(BF16)
            acc_ref[mi, :, qs] = (acc_ref[mi, :, qs] * alpha
                                  + jnp.dot(vt_ref[c], p, preferred_element_type=F32))
            m_ref[mi, :, qs] = m_new
        return [functools.partial(one, *col) for col in columns(diag)]

    def emit(scores, updates):
        scores = list(scores)
        for s in scores[:QK_AHEAD]:
            s()
        rest = scores[QK_AHEAD:]
        for u in updates:
            u()
            if rest:
                rest.pop(0)()
        for s in rest:
            s()

    assert TQ // TK == 2
    c0 = 2 * qi

    @pl.when(qi > 0)
    def _():
        qt_ref[...] = qtn_ref[...]

        def body(i, carry):
            emit(score(2 * i + 1, 1, None), update(2 * i, 0, None))
            emit(score(2 * i + 2, 0, None), update(2 * i + 1, 1, None))
            return carry

        lax.fori_loop(0, qi - 1, body, 0)
        emit(score(c0 - 1, 1, None), update(c0 - 2, 0, None))
        emit(score(c0, 0, 0), update(c0 - 1, 1, None))

    @pl.when(qi == 0)
    def _():
        store_qt(qt_ref, q_ref)
        emit(score(c0, 0, 0), [])

    store_qt(qtn_ref, qn_ref)
    emit(score(c0 + 1, 1, 1), update(c0, 0, 0))
    last_tile = qi == pl.num_programs(2) - 1

    @pl.when(jnp.logical_not(last_tile))
    def _():
        emit(score(0, 0, None, qtn_ref), update(c0 + 1, 1, 1))

    @pl.when(last_tile)
    def _():
        emit([], update(c0 + 1, 1, 1))

    lam = (jnp.exp(jnp.sum(lq1_ref[...] * lk1_ref[...], axis=-1, keepdims=True))
           - jnp.exp(jnp.sum(lq2_ref[...] * lk2_ref[...], axis=-1, keepdims=True)) + lam_init)
    o1 = acc_ref[0, :DA_V_DIM, :] / acc_ref[0, DA_V_DIM:DA_V_DIM + 1, :]
    o2 = acc_ref[1, :DA_V_DIM, :] / acc_ref[1, DA_V_DIM:DA_V_DIM + 1, :]
    a = o1 - lam * o2
    y = a * lax.rsqrt(jnp.mean(a * a, axis=0, keepdims=True) + EPS) * g_ref[...]
    o_ref[...] = (y * (1.0 - lam_init)).astype(BF16)


def _diff_attn(proj, lq1, lk1, lq2, lk2, g, lam_init, batch, seq, layer):
    n = proj.shape[0]
    nq = seq // TQ
    kcol, vcol = COL_K_DA // LANES, COL_V_DA // LANES
    return pl.pallas_call(
        functools.partial(_diff_attn_kernel, lam_init=lam_init),
        grid=(batch, DA_HEADS, nq),
        in_specs=[
            pl.BlockSpec((TQ, LANES), lambda b, h, i: (b * nq + i, h)),
            pl.BlockSpec((TQ, LANES), lambda b, h, i: (b * nq + jnp.minimum(i + 1, nq - 1), h)),
            pl.BlockSpec((seq, LANES), lambda b, h, i: (b, kcol + h)),
            pl.BlockSpec((seq, LANES), lambda b, h, i: (b, vcol + h)),
            _layer_spec(lq1, layer), _layer_spec(lk1, layer), _layer_spec(lq2, layer), _layer_spec(lk2, layer),
            _layer_spec(g, layer),
        ],
        out_specs=pl.BlockSpec((LANES, TQ), lambda b, h, i: (h, b * nq + i)),
        out_shape=jax.ShapeDtypeStruct((DA_WIDTH, n), BF16),
        scratch_shapes=[
            pltpu.VMEM((seq // TK, VT_ROWS, TK), BF16),
            pltpu.VMEM((2, LANES, TQ), BF16),
            pltpu.VMEM((2, LANES, TQ), BF16),
            pltpu.VMEM((2 * TQ // SUB, TK, SUB), F32),
            pltpu.VMEM((2 * TQ // SUB, TK, SUB), F32),
            pltpu.VMEM((2 * TQ // SUB, 1, SUB), F32),
            pltpu.VMEM((2 * TQ // SUB, 1, SUB), F32),
            pltpu.VMEM((2, 1, TQ), F32),
            pltpu.VMEM((2, VT_ROWS, TQ), F32),
        ],
        compiler_params=_params("arbitrary", "arbitrary", "arbitrary"),
        name="diff_attn",
    )(proj, proj, proj, proj, lq1, lk1, lq2, lk2, g)


def _split_dot(x, w):
    hi = x.astype(BF16)
    lo = (x - hi.astype(F32)).astype(BF16)
    return (jnp.dot(hi, w, preferred_element_type=F32) + jnp.dot(lo, w, preferred_element_type=F32))


def _ret_pool_kernel(qk_ref, v_ref, g_ref, u_ref, decay_ref, xi_ref, zeta_ref, cd_ref, bd_ref,
                     avg_ref, retg_ref, win_ref, pw_ref, ps_ref, o_ref,
                     state_ref, halo_ref, oret_ref):
    i = pl.program_id(1)

    @pl.when(i == 0)
    def _():
        state_ref[...] = jnp.zeros(state_ref.shape, F32)
        halo_ref[...] = jnp.zeros(halo_ref.shape, F32)

    C = RET_CHUNK
    chunks = range(R_RET // C)
    qlane = lax.broadcasted_iota(jnp.int32, (C, LANES), 1) // RET_QK_DIM
    vlane = lax.broadcasted_iota(jnp.int32, (C, RET_WIDTH), 1) // RET_V_DIM
    qs = [qk_ref[c * C:(c + 1) * C, :LANES] for c in chunks]
    ks = [qk_ref[c * C:(c + 1) * C, LANES:] for c in chunks]
    vs = [v_ref[c * C:(c + 1) * C, :] for c in chunks]
    inner = [[lax.dot_general(jnp.where(qlane == h, qs[c], jnp.zeros_like(qs[c])), ks[c],
                              (((1,), (1,)), ((), ())), preferred_element_type=F32)
              for h in range(RET_HEADS)] for c in chunks]
    upd = [lax.dot_general((ks[c].astype(F32) * zeta_ref[...]).astype(BF16), vs[c],
                           (((0,), (0,)), ((), ())), preferred_element_type=F32) for c in chunks]
    states = [state_ref[...]]
    for c in chunks:
        states.append(states[c] * cd_ref[...] + bd_ref[...] * upd[c])
    state_ref[...] = states[-1]
    for c in chunks:
        o_c = jnp.dot((qs[c].astype(F32) * xi_ref[...]).astype(BF16), states[c].astype(BF16),
                      preferred_element_type=F32)
        for h in range(RET_HEADS):
            vh = jnp.where(vlane == h, vs[c], jnp.zeros_like(vs[c]))
            o_c = o_c + jnp.dot((inner[c][h] * decay_ref[h]).astype(BF16), vh, preferred_element_type=F32)
        oret_ref[c * C:(c + 1) * C, :] = o_c

    o = oret_ref[...]
    mu = _split_dot(o, avg_ref[...])
    d = o - mu
    var = jnp.dot((d * d).astype(BF16), avg_ref[...], preferred_element_type=F32)
    y = d * lax.rsqrt(var + EPS) * retg_ref[...]
    gate = g_ref[...].astype(F32)
    o_ref[:, :RET_WIDTH] = (gate * jax.nn.sigmoid(gate) * y).astype(BF16)

    u = u_ref[...].astype(F32)
    halo_groups = POOL_HALO // SUBLANES
    ext = jnp.concatenate([halo_ref[...], u], axis=0).reshape(-1, SUBLANES, POOL_WIDTH)
    halo_ref[...] = u[R_RET - POOL_HALO:, :]

    def shift_down(x, k):
        padded = jnp.concatenate([jnp.zeros_like(x[:1]), x], axis=0)
        if k == SUBLANES:
            return padded[:-1]
        r = pltpu.roll(padded, k, 1)
        return jnp.where(lax.broadcasted_iota(jnp.int32, x.shape, 1) < k, r[:-1], r[1:])

    assert POOL_WINDOWS == (2, 4, 8, 16) and POOL_DIM * 2 == LANES
    s2 = ext + shift_down(ext, 1)
    s4 = s2 + shift_down(s2, 2)
    s4_hi = s4[:, :, LANES:]
    s8 = s4_hi + shift_down(s4_hi, 4)
    s16 = s8 + shift_down(s8, 8)
    first = lax.broadcasted_iota(jnp.int32, s8.shape, 2) < POOL_DIM
    psum = jnp.concatenate([jnp.where(first, s2[:, :, :LANES], s4[:, :, :LANES]), jnp.where(first, s8, s16)],
                           axis=2)[halo_groups:].reshape(R_RET, POOL_WIDTH)
    t = (i * R_RET + lax.broadcasted_iota(jnp.int32, u.shape, 0)).astype(F32)
    pooled = psum / jnp.minimum(t + 1.0, win_ref[...]) - u
    yp = jnp.dot(pooled.astype(BF16), pw_ref[...], preferred_element_type=F32) * ps_ref[...]
    o_ref[:, RET_WIDTH:] = yp.astype(BF16)


def _ret_pool(proj, tabs, ret_g, pool_w_bd, pool_scale, batch, seq, layer):
    n = proj.shape[0]
    nr = seq // R_RET
    decay, xi, zeta, cd, bd, avg, win = tabs

    def col(c):
        return pl.BlockSpec((R_RET, MXU_COLS), lambda b, i: (b * nr + i, c // MXU_COLS))

    def const(shape):
        return pl.BlockSpec(shape, lambda b, i: (0,) * len(shape))

    return pl.pallas_call(
        _ret_pool_kernel,
        grid=(batch, nr),
        in_specs=[
            col(COL_QK_R), col(COL_V_R), col(COL_G_R), col(COL_U),
            const(decay.shape), const(xi.shape), const(zeta.shape), const(cd.shape), const(bd.shape),
            const(avg.shape), _layer_spec(ret_g, layer), const(win.shape), _layer_spec(pool_w_bd, layer),
            _layer_spec(pool_scale, layer),
        ],
        out_specs=pl.BlockSpec((R_RET, RET_WIDTH + POOL_WIDTH), lambda b, i: (b * nr + i, 0)),
        out_shape=jax.ShapeDtypeStruct((n, RET_WIDTH + POOL_WIDTH), BF16),
        scratch_shapes=[
            pltpu.VMEM((LANES, RET_WIDTH), F32),
            pltpu.VMEM((POOL_HALO, POOL_WIDTH), F32),
            pltpu.VMEM((R_RET, RET_WIDTH), F32),
        ],
        compiler_params=_params("arbitrary", "arbitrary"),
        name="ret_pool",
    )(proj, proj, proj, proj, decay, xi, zeta, cd, bd, avg, ret_g, win, pool_w_bd, pool_scale)


def _mix_up_kernel(oda_ref, orp_ref, x_ref, wo_ref, gpost_ref, gpre_ref, w_ref, cw_ref, cb_ref,
                   xo_ref, a_ref, carry_ref, ext_ref, *, tiles_per_seq):
    i = pl.program_id(0)
    tm = x_ref.shape[0]

    @pl.when(i % tiles_per_seq == 0)
    def _():
        carry_ref[...] = jnp.zeros(carry_ref.shape, F32)

    mix = (lax.dot_general(oda_ref[...], wo_ref[:DA_WIDTH, :], (((0,), (0,)), ((), ())),
                           preferred_element_type=F32)
           + jnp.dot(orp_ref[...], wo_ref[DA_WIDTH:, :], preferred_element_type=F32))
    xn = x_ref[...] + _rms(mix, gpost_ref[...])
    xo_ref[...] = xn
    h = _rms(xn, gpre_ref[...]).astype(BF16)

    def conv_cols(slot, c0):
        u = jnp.dot(h, w_ref[:, c0:c0 + MXU_COLS], preferred_element_type=F32)
        ext_ref[slot, :SUBLANES, :] = carry_ref[:, c0:c0 + MXU_COLS]
        ext_ref[slot, SUBLANES:, :] = u
        carry_ref[:, c0:c0 + MXU_COLS] = u[tm - SUBLANES:, :]
        cw = cw_ref[:, c0:c0 + MXU_COLS]
        return (cb_ref[:, c0:c0 + MXU_COLS]
                + cw[0:1] * ext_ref[slot, SUBLANES - 2:SUBLANES - 2 + tm, :]
                + cw[1:2] * ext_ref[slot, SUBLANES - 1:SUBLANES - 1 + tm, :]
                + cw[2:3] * u)

    for c in range(D_FF // MXU_COLS):
        c0 = c * MXU_COLS
        g = conv_cols(0, c0)
        v = conv_cols(1, D_FF + c0)
        th = jnp.tanh(g * (g * g * (GELU_C * 0.044715) + GELU_C))
        a_ref[:, c0:c0 + MXU_COLS] = (g * v * (th + 1.0)).astype(BF16)


def _mix_up(o_da, o_rp, x, w_out, g_post, g_pre, w_up, cw, cb, seq, layer):
    n = x.shape[0]
    row = lambda width: pl.BlockSpec((TM_MIX, width), lambda i: (i, 0))

    def resident(arr):
        tail = arr.shape[1:]
        return pl.BlockSpec((None,) + tail, lambda *_: (layer,) + (0,) * len(tail),
                            pipeline_mode=pl.Buffered(1))

    return pl.pallas_call(
        functools.partial(_mix_up_kernel, tiles_per_seq=seq // TM_MIX),
        grid=(n // TM_MIX,),
        in_specs=[pl.BlockSpec((DA_WIDTH, TM_MIX), lambda i: (0, i)), row(RET_WIDTH + POOL_WIDTH), row(D_MODEL),
                  resident(w_out), _layer_spec(g_post, layer), _layer_spec(g_pre, layer),
                  resident(w_up), _layer_spec(cw, layer), _layer_spec(cb, layer)],
        out_specs=[row(D_MODEL), row(D_FF)],
        out_shape=[jax.ShapeDtypeStruct((n, D_MODEL), F32), jax.ShapeDtypeStruct((n, D_FF), BF16)],
        scratch_shapes=[
            pltpu.VMEM((SUBLANES, 2 * D_FF), F32),
            pltpu.VMEM((2, TM_MIX + SUBLANES, MXU_COLS), F32),
        ],
        compiler_params=_params("arbitrary"),
        name="mix_up",
    )(o_da, o_rp, x, w_out, g_post, g_pre, w_up, cw, cb)


def _down_proj_kernel(a_ref, x_ref, w_ref, g_ref, xo_ref):
    y = jnp.dot(a_ref[...], w_ref[...], preferred_element_type=F32)
    xo_ref[...] = x_ref[...] + _rms(y, g_ref[...])


def _down_proj(a, x, w, g, layer):
    n = x.shape[0]
    row = lambda width: pl.BlockSpec((TM_PROJ, width), lambda i: (i, 0))
    return pl.pallas_call(
        _down_proj_kernel,
        grid=(n // TM_PROJ,),
        in_specs=[row(D_FF), row(D_MODEL), _layer_spec(w, layer), _layer_spec(g, layer)],
        out_specs=row(D_MODEL),
        out_shape=jax.ShapeDtypeStruct((n, D_MODEL), F32),
        compiler_params=_params("arbitrary"),
        name="down_proj",
    )(a, x, w, g)


def _rotary_tables(seq, rot_dim, theta, period):
    inv = jnp.float32(theta) ** (-jnp.arange(0, rot_dim, 2, dtype=F32) / rot_dim)
    ang = jnp.arange(seq, dtype=F32)[:, None] * inv[None, :]
    cos, sin = jnp.cos(ang), jnp.sin(ang)
    half = rot_dim // 2
    zh = jnp.zeros((seq, half), F32)
    zp = jnp.zeros((seq, period - rot_dim), F32)
    return jnp.concatenate([cos, cos, jnp.ones_like(zp), zh, sin, zp, -sin, zh, zp], axis=1)


def _retention_tables():
    H, C = RET_HEADS, RET_CHUNK
    f32 = np.float32
    log_g = np.log(f32(1.0) - f32(2.0) ** (f32(-5.0) - np.arange(H, dtype=f32)))
    idx = np.arange(C, dtype=f32)
    diff = idx[:, None] - idx[None, :]
    decay = np.where(diff >= 0, np.exp(np.maximum(diff, f32(0.0)) * log_g[:, None, None]), f32(0.0))
    xi = np.exp((idx + f32(1.0)) * log_g[:, None])
    zeta = np.exp((f32(C) - f32(1.0) - idx) * log_g[:, None])
    chunk_decay = np.exp(f32(C) * log_g)
    xi_t = np.repeat(xi.T, RET_QK_DIM, axis=1)
    zeta_t = np.repeat(zeta.T, RET_QK_DIM, axis=1)
    row_head = np.arange(LANES) // RET_QK_DIM
    col_head = np.arange(RET_WIDTH) // RET_V_DIM
    bd = (row_head[:, None] == col_head[None, :]).astype(f32)
    cd = bd * chunk_decay[row_head][:, None]
    avg = (col_head[:, None] == col_head[None, :]).astype(f32) / f32(RET_V_DIM)
    win = np.repeat(np.asarray(POOL_WINDOWS, f32), POOL_DIM)[None, :]
    tabs = [jnp.asarray(t.astype(f32)) for t in (decay, xi_t, zeta_t, cd, bd)]
    return (*tabs, jnp.asarray(avg, BF16), jnp.asarray(win))


def _block_diag(w):
    L, G, P, _ = w.shape
    eye = jnp.eye(G, dtype=w.dtype)
    return (eye[None, :, None, :, None] * w[:, :, :, None, :]).reshape(L, G * P, G * P)


def kernel(x, norm_mix_pre, norm_mix_post, w_in, lambda_q1, lambda_k1, lambda_q2, lambda_k2, diff_subln,
           ret_norm, pool_w, pool_scale, w_out, norm_mlp_pre, norm_mlp_post, w_up, conv_w, conv_b, w_down):
    batch, seq, _ = x.shape
    depth = w_in.shape[0]
    assert seq % TQ == 0 and seq % TM_PROJ == 0 and seq % TM_MIX == 0 and seq % R_RET == 0
    xf = x.reshape(batch * seq, D_MODEL)

    tda = _rotary_tables(seq, DA_ROT_DIM, ROPE_THETA, DA_QK_DIM)
    tret = _rotary_tables(seq, RET_QK_DIM, RET_THETA, RET_QK_DIM)
    rtabs = _retention_tables()

    rows = lambda a: a.reshape(depth, 1, -1)
    w_in_b, w_out_b, w_up_b, w_down_b = (w.astype(BF16) for w in (w_in, w_out, w_up, w_down))
    pool_w_b = _block_diag(pool_w).astype(BF16)
    half_val = jnp.concatenate([jnp.ones((D_FF,), F32), jnp.full((D_FF,), 0.5, F32)])
    conv_w_s, conv_b_s = conv_w * half_val, rows(conv_b * half_val)
    g_mix_pre, g_mix_post, g_mlp_pre, g_mlp_post = (rows(g) for g in (norm_mix_pre, norm_mix_post,
                                                                      norm_mlp_pre, norm_mlp_post))
    lq1, lk1, lq2, lk2 = (rows(v) for v in (lambda_q1, lambda_k1, lambda_q2, lambda_k2))
    g_sub = diff_subln.reshape(depth, DA_V_DIM, 1)
    g_ret, p_scale = rows(ret_norm), rows(pool_scale)

    for l in range(depth):
        lam_init = 0.8 - 0.6 * math.exp(-0.3 * l)
        proj = _in_proj(xf, g_mix_pre, w_in_b, tda, tret, seq, l)
        o_da = _diff_attn(proj, lq1, lk1, lq2, lk2, g_sub, lam_init, batch, seq, l)
        o_rp = _ret_pool(proj, rtabs, g_ret, pool_w_b, p_scale, batch, seq, l)
        xf, a = _mix_up(o_da, o_rp, xf, w_out_b, g_mix_post, g_mlp_pre, w_up_b, conv_w_s, conv_b_s, seq, l)
        xf = _down_proj(a, xf, w_down_b, g_mlp_post, l)
    return xf.reshape(batch, seq, D_MODEL)
```

```python
import functools
import math

import jax
import jax.numpy as jnp
import numpy as np
from jax import lax
from jax.experimental import pallas as pl
from jax.experimental.pallas import tpu as pltpu

F32 = jnp.float32
BF16 = jnp.bfloat16

D_MODEL = 1024
DA_HEADS = 4
DA_QK_DIM = 64
DA_V_DIM = 128
DA_ROT_DIM = 16
ROPE_THETA = 500000.0
RET_HEADS = 4
RET_QK_DIM = 32
RET_V_DIM = 64
RET_THETA = 10000.0
RET_CHUNK = 128
POOL_GROUPS = 4
POOL_DIM = 64
POOL_WINDOWS = (2, 4, 8, 16)
POOL_HALO = 16
DA_WIDTH = 512
RET_WIDTH = 256
POOL_WIDTH = 256
IN_WIDTH = 2560
D_FF = 2816
CONV_WIDTH = 3
EPS = 1e-6

COL_Q_DA, COL_K_DA, COL_V_DA = 0, 512, 1024
COL_QK_R, COL_V_R, COL_G_R, COL_U = 1536, 1792, 2048, 2304

LANES = 128
SUBLANES = 8
MXU_COLS = 256
VMEM_LIMIT = 56 * 1024 * 1024

TM_PROJ = 1024
TM_MIX = 512
TQ = 1024
TK = 512
SUB = 256
QK_AHEAD = 2
VT_ROWS = DA_V_DIM + 16
LOG2E = math.log2(math.e)
GELU_C = math.sqrt(2.0 / math.pi)


def _params(*sem):
    return pltpu.CompilerParams(dimension_semantics=sem, vmem_limit_bytes=VMEM_LIMIT)


def _layer_spec(arr, layer):
    tail = arr.shape[1:]
    return pl.BlockSpec((None,) + tail, lambda *_: (layer,) + (0,) * len(tail))


def _rms(x, g):
    return x * lax.rsqrt(jnp.mean(x * x, axis=-1, keepdims=True) + EPS) * g


def _rot(x, tab_ref, shift):
    return (x * tab_ref[:, :LANES]
            + pltpu.roll(x, shift, 1) * tab_ref[:, LANES:2 * LANES]
            + pltpu.roll(x, LANES - shift, 1) * tab_ref[:, 2 * LANES:])


def _in_mix_kernel(x_ref, g_ref, w_ref, tda_ref, tret_ref, *rest, tiles_per_seq):
    ret_consts, (o_ref, orp_ref), ret_scratch = rest[:10], rest[10:12], rest[12:]
    i = pl.program_id(0)
    h = _rms(x_ref[...], g_ref[...]).astype(BF16)
    kept = {}

    def column(c0):
        p = jnp.dot(h, w_ref[:, c0:c0 + MXU_COLS], preferred_element_type=F32)
        lo, hi = p[:, :LANES], p[:, LANES:]
        if c0 < COL_V_DA:
            lo = _rot(lo, tda_ref, DA_ROT_DIM // 2)
            hi = _rot(hi, tda_ref, DA_ROT_DIM // 2)
            if c0 < COL_K_DA:
                lo = lo * (DA_QK_DIM ** -0.5 * LOG2E)
                hi = hi * (DA_QK_DIM ** -0.5 * LOG2E)
        elif c0 == COL_QK_R:
            lo = _rot(lo, tret_ref, RET_QK_DIM // 2)
            hi = _rot(hi, tret_ref, RET_QK_DIM // 2) * (RET_QK_DIM ** -0.5)
        if c0 < COL_QK_R:
            o_ref[:, c0:c0 + LANES] = lo.astype(BF16)
            o_ref[:, c0 + LANES:c0 + MXU_COLS] = hi.astype(BF16)
        else:
            kept[c0] = jnp.concatenate([lo.astype(BF16), hi.astype(BF16)], axis=1)

    for c0 in range(COL_QK_R, IN_WIDTH, MXU_COLS):
        column(c0)
    fillers = [functools.partial(column, c0) for c0 in range(0, COL_QK_R, MXU_COLS)]
    _ret_pool_body(i % tiles_per_seq == 0, (i % tiles_per_seq) * x_ref.shape[0],
                   kept[COL_QK_R], kept[COL_V_R], kept[COL_G_R], kept[COL_U],
                   *ret_consts, orp_ref, *ret_scratch, fillers=fillers)
    for f in fillers:
        f()


def _in_mix(x, g, w, tda, tret, rtabs, ret_g, pool_w_bd, pool_scale, seq, layer):
    n = x.shape[0]
    tiles_per_seq = seq // TM_PROJ
    decay, xi, zeta, cd, bd, avg, win = rtabs
    const = lambda a: pl.BlockSpec(a.shape, lambda i: (0,) * a.ndim)
    row = lambda width: pl.BlockSpec((TM_PROJ, width), lambda i: (i, 0))
    return pl.pallas_call(
        functools.partial(_in_mix_kernel, tiles_per_seq=tiles_per_seq),
        grid=(n // TM_PROJ,),
        in_specs=[
            row(D_MODEL),
            _layer_spec(g, layer),
            _layer_spec(w, layer),
            pl.BlockSpec((TM_PROJ, tda.shape[1]), lambda i: (i % tiles_per_seq, 0)),
            pl.BlockSpec((TM_PROJ, tret.shape[1]), lambda i: (i % tiles_per_seq, 0)),
            const(decay), const(xi), const(zeta), const(cd), const(bd), const(avg),
            _layer_spec(ret_g, layer), const(win), _layer_spec(pool_w_bd, layer), _layer_spec(pool_scale, layer),
        ],
        out_specs=[row(COL_QK_R), row(RET_WIDTH + POOL_WIDTH)],
        out_shape=[jax.ShapeDtypeStruct((n, COL_QK_R), BF16),
                   jax.ShapeDtypeStruct((n, RET_WIDTH + POOL_WIDTH), BF16)],
        scratch_shapes=[
            pltpu.VMEM((LANES, RET_WIDTH), F32),
            pltpu.VMEM((POOL_HALO, POOL_WIDTH), F32),
            pltpu.VMEM((TM_PROJ, RET_WIDTH), F32),
        ],
        compiler_params=_params("arbitrary"),
        name="in_mix",
    )(x, g, w, tda, tret, decay, xi, zeta, cd, bd, avg, ret_g, win, pool_w_bd, pool_scale)


def _diff_attn_kernel(q_ref, qn_ref, k_ref, v_ref, lq1_ref, lk1_ref, lq2_ref, lk2_ref, g_ref, o_ref,
                      vt_ref, qt_ref, qtn_ref, s0_ref, s1_ref, mx0_ref, mx1_ref, m_ref, acc_ref, *, lam_init):
    qi = pl.program_id(2)
    nkv = v_ref.shape[0] // TK
    s_refs, mx_refs = (s0_ref, s1_ref), (mx0_ref, mx1_ref)

    @pl.when(qi == 0)
    def _():
        for c in range(nkv):
            vt_ref[c, :DA_V_DIM, :] = v_ref[c * TK:(c + 1) * TK, :].astype(F32).T.astype(BF16)
            vt_ref[c, DA_V_DIM:, :] = jnp.ones((VT_ROWS - DA_V_DIM, TK), BF16)

    def store_qt(dst_ref, src_ref):
        qt = src_ref[...].astype(F32).T
        feat = lax.broadcasted_iota(jnp.int32, qt.shape, 0)
        dst_ref[0] = jnp.where(feat < DA_QK_DIM, qt, 0.0).astype(BF16)
        dst_ref[1] = jnp.where(feat >= DA_QK_DIM, qt, 0.0).astype(BF16)

    m_ref[...] = jnp.full(m_ref.shape, -jnp.inf, F32)
    acc_ref[...] = jnp.zeros(acc_ref.shape, F32)

    def columns(diag):
        out = []
        for qb in range(TQ // SUB):
            if diag is not None and diag * TK > (qb + 1) * SUB - 1:
                continue
            mask = diag is not None and (diag + 1) * TK - 1 > qb * SUB
            out += [(2 * qb + mi, qb, mi, mask) for mi in range(2)]
        return out

    def score(c, slot, diag, q_src=qt_ref):
        def one(j, qb, mi, mask):
            ks = k_ref[pl.ds(pl.multiple_of(c * TK, TK), TK), :]
            st = jnp.dot(ks, q_src[mi, :, qb * SUB:(qb + 1) * SUB],
                         preferred_element_type=F32)
            if mask:
                key = diag * TK + lax.broadcasted_iota(jnp.int32, st.shape, 0)
                qry = qb * SUB + lax.broadcasted_iota(jnp.int32, st.shape, 1)
                st = jnp.where(key <= qry, st, -jnp.inf)
            s_refs[slot][j] = st
            mx_refs[slot][j] = jnp.max(st, axis=0, keepdims=True)
        return [functools.partial(one, *col) for col in columns(diag)]

    def update(c, slot, diag):
        def one(j, qb, mi, mask):
            qs = slice(qb * SUB, (qb + 1) * SUB)
            m_prev = m_ref[mi, :, qs]
            m_new = jnp.maximum(m_prev, mx_refs[slot][j])
            alpha = jnp.exp2(m_prev - m_new)
            p = jnp.exp2(s_refs[slot][j] - m_new).astype(BF16)
            acc_ref[mi, :, qs] = (acc_ref[mi, :, qs] * alpha
                                  + jnp.dot(vt_ref[c], p, preferred_element_type=F32))
            m_ref[mi, :, qs] = m_new
        return [functools.partial(one, *col) for col in columns(diag)]

    def emit(scores, updates):
        scores = list(scores)
        for s in scores[:QK_AHEAD]:
            s()
        rest = scores[QK_AHEAD:]
        for u in updates:
            u()
            if rest:
                rest.pop(0)()
        for s in rest:
            s()

    assert TQ // TK == 2
    c0 = 2 * qi

    @pl.when(qi > 0)
    def _():
        qt_ref[...] = qtn_ref[...]

        def body(i, carry):
            emit(score(2 * i + 1, 1, None), update(2 * i, 0, None))
            emit(score(2 * i + 2, 0, None), update(2 * i + 1, 1, None))
            return carry

        lax.fori_loop(0, qi - 1, body, 0)
        emit(score(c0 - 1, 1, None), update(c0 - 2, 0, None))
        emit(score(c0, 0, 0), update(c0 - 1, 1, None))

    @pl.when(qi == 0)
    def _():
        store_qt(qt_ref, q_ref)
        emit(score(c0, 0, 0), [])

    store_qt(qtn_ref, qn_ref)
    emit(score(c0 + 1, 1, 1), update(c0, 0, 0))
    last_tile = qi == pl.num_programs(2) - 1

    @pl.when(jnp.logical_not(last_tile))
    def _():
        emit(score(0, 0, None, qtn_ref), update(c0 + 1, 1, 1))

    @pl.when(last_tile)
    def _():
        emit([], update(c0 + 1, 1, 1))

    lam = (jnp.exp(jnp.sum(lq1_ref[...] * lk1_ref[...], axis=-1, keepdims=True))
           - jnp.exp(jnp.sum(lq2_ref[...] * lk2_ref[...], axis=-1, keepdims=True)) + lam_init)
    o1 = acc_ref[0, :DA_V_DIM, :] / acc_ref[0, DA_V_DIM:DA_V_DIM + 1, :]
    o2 = acc_ref[1, :DA_V_DIM, :] / acc_ref[1, DA_V_DIM:DA_V_DIM + 1, :]
    a = o1 - lam * o2
    y = a * lax.rsqrt(jnp.mean(a * a, axis=0, keepdims=True) + EPS) * g_ref[...]
    o_ref[...] = (y * (1.0 - lam_init)).astype(BF16)


def _diff_attn(proj, lq1, lk1, lq2, lk2, g, lam_init, batch, seq, layer):
    n = proj.shape[0]
    nq = seq // TQ
    kcol, vcol = COL_K_DA // LANES, COL_V_DA // LANES
    return pl.pallas_call(
        functools.partial(_diff_attn_kernel, lam_init=lam_init),
        grid=(batch, DA_HEADS, nq),
        in_specs=[
            pl.BlockSpec((TQ, LANES), lambda b, h, i: (b * nq + i, h)),
            pl.BlockSpec((TQ, LANES), lambda b, h, i: (b * nq + jnp.minimum(i + 1, nq - 1), h)),
            pl.BlockSpec((seq, LANES), lambda b, h, i: (b, kcol + h)),
            pl.BlockSpec((seq, LANES), lambda b, h, i: (b, vcol + h)),
            _layer_spec(lq1, layer), _layer_spec(lk1, layer), _layer_spec(lq2, layer), _layer_spec(lk2, layer),
            _layer_spec(g, layer),
        ],
        out_specs=pl.BlockSpec((LANES, TQ), lambda b, h, i: (h, b * nq + i)),
        out_shape=jax.ShapeDtypeStruct((DA_WIDTH, n), BF16),
        scratch_shapes=[
            pltpu.VMEM((seq // TK, VT_ROWS, TK), BF16),
            pltpu.VMEM((2, LANES, TQ), BF16),
            pltpu.VMEM((2, LANES, TQ), BF16),
            pltpu.VMEM((2 * TQ // SUB, TK, SUB), F32),
            pltpu.VMEM((2 * TQ // SUB, TK, SUB), F32),
            pltpu.VMEM((2 * TQ // SUB, 1, SUB), F32),
            pltpu.VMEM((2 * TQ // SUB, 1, SUB), F32),
            pltpu.VMEM((2, 1, TQ), F32),
            pltpu.VMEM((2, VT_ROWS, TQ), F32),
        ],
        compiler_params=_params("arbitrary", "arbitrary", "arbitrary"),
        name="diff_attn",
    )(proj, proj, proj, proj, lq1, lk1, lq2, lk2, g)


def _split_dot(x, w):
    hi = x.astype(BF16)
    lo = (x - hi.astype(F32)).astype(BF16)
    return (jnp.dot(hi, w, preferred_element_type=F32) + jnp.dot(lo, w, preferred_element_type=F32))


def _ret_pool_body(seq_start, t0, qk, v, g, u, decay_ref, xi_ref, zeta_ref, cd_ref, bd_ref,
                   avg_ref, retg_ref, win_ref, pw_ref, ps_ref, o_ref, state_ref, halo_ref, oret_ref, *, fillers):
    rows = qk.shape[0]

    def fill():
        if fillers:
            fillers.pop(0)()

    @pl.when(seq_start)
    def _():
        state_ref[...] = jnp.zeros(state_ref.shape, F32)
        halo_ref[...] = jnp.zeros(halo_ref.shape, F32)

    C = RET_CHUNK
    chunks = range(rows // C)
    qlane = lax.broadcasted_iota(jnp.int32, (C, LANES), 1) // RET_QK_DIM
    vlane = lax.broadcasted_iota(jnp.int32, (C, RET_WIDTH), 1) // RET_V_DIM
    qs = [qk[c * C:(c + 1) * C, :LANES] for c in chunks]
    ks = [qk[c * C:(c + 1) * C, LANES:] for c in chunks]
    vs = [v[c * C:(c + 1) * C, :] for c in chunks]
    inner = [[lax.dot_general(jnp.where(qlane == h, qs[c], jnp.zeros_like(qs[c])), ks[c],
                              (((1,), (1,)), ((), ())), preferred_element_type=F32)
              for h in range(RET_HEADS)] for c in chunks]
    upd = [lax.dot_general((ks[c].astype(F32) * zeta_ref[...]).astype(BF16), vs[c],
                           (((0,), (0,)), ((), ())), preferred_element_type=F32) for c in chunks]
    fill()
    states = [state_ref[...]]
    for c in chunks:
        states.append(states[c] * cd_ref[...] + bd_ref[...] * upd[c])
    state_ref[...] = states[-1]
    for c in chunks:
        o_c = jnp.dot((qs[c].astype(F32) * xi_ref[...]).astype(BF16), states[c].astype(BF16),
                      preferred_element_type=F32)
        for h in range(RET_HEADS):
            vh = jnp.where(vlane == h, vs[c], jnp.zeros_like(vs[c]))
            o_c = o_c + jnp.dot((inner[c][h] * decay_ref[h]).astype(BF16), vh, preferred_element_type=F32)
        oret_ref[c * C:(c + 1) * C, :] = o_c
        if c % 2 == 1:
            fill()

    o = oret_ref[...]
    mu = _split_dot(o, avg_ref[...])
    d = o - mu
    var = jnp.dot((d * d).astype(BF16), avg_ref[...], preferred_element_type=F32)
    y = d * lax.rsqrt(var + EPS) * retg_ref[...]
    gate = g.astype(F32)
    o_ref[:, :RET_WIDTH] = (gate * jax.nn.sigmoid(gate) * y).astype(BF16)
    fill()

    u = u.astype(F32)
    halo_groups = POOL_HALO // SUBLANES
    ext = jnp.concatenate([halo_ref[...], u], axis=0).reshape(-1, SUBLANES, POOL_WIDTH)
    halo_ref[...] = u[rows - POOL_HALO:, :]

    def shift_down(x, k):
        padded = jnp.concatenate([jnp.zeros_like(x[:1]), x], axis=0)
        if k == SUBLANES:
            return padded[:-1]
        r = pltpu.roll(padded, k, 1)
        return jnp.where(lax.broadcasted_iota(jnp.int32, x.shape, 1) < k, r[:-1], r[1:])

    assert POOL_WINDOWS == (2, 4, 8, 16) and POOL_DIM * 2 == LANES
    s2 = ext + shift_down(ext, 1)
    s4 = s2 + shift_down(s2, 2)
    s4_hi = s4[:, :, LANES:]
    s8 = s4_hi + shift_down(s4_hi, 4)
    s16 = s8 + shift_down(s8, 8)
    first = lax.broadcasted_iota(jnp.int32, s8.shape, 2) < POOL_DIM
    psum = jnp.concatenate([jnp.where(first, s2[:, :, :LANES], s4[:, :, :LANES]), jnp.where(first, s8, s16)],
                           axis=2)[halo_groups:].reshape(rows, POOL_WIDTH)
    t = (t0 + lax.broadcasted_iota(jnp.int32, u.shape, 0)).astype(F32)
    pooled = psum / jnp.minimum(t + 1.0, win_ref[...]) - u
    yp = jnp.dot(pooled.astype(BF16), pw_ref[...], preferred_element_type=F32) * ps_ref[...]
    o_ref[:, RET_WIDTH:] = yp.astype(BF16)


def _mix_up_kernel(oda_ref, orp_ref, x_ref, wo_ref, gpost_ref, gpre_ref, w_ref, cw_ref, cb_ref,
                   xo_ref, a_ref, carry_ref, ext_ref, *, tiles_per_seq):
    i = pl.program_id(0)
    tm = x_ref.shape[0]

    @pl.when(i % tiles_per_seq == 0)
    def _():
        carry_ref[...] = jnp.zeros(carry_ref.shape, F32)

    mix = (lax.dot_general(oda_ref[...], wo_ref[:DA_WIDTH, :], (((0,), (0,)), ((), ())),
                           preferred_element_type=F32)
           + jnp.dot(orp_ref[...], wo_ref[DA_WIDTH:, :], preferred_element_type=F32))
    xn = x_ref[...] + _rms(mix, gpost_ref[...])
    xo_ref[...] = xn
    h = _rms(xn, gpre_ref[...]).astype(BF16)

    def conv_cols(slot, c0):
        u = jnp.dot(h, w_ref[:, c0:c0 + MXU_COLS], preferred_element_type=F32)
        ext_ref[slot, :SUBLANES, :] = carry_ref[:, c0:c0 + MXU_COLS]
        ext_ref[slot, SUBLANES:, :] = u
        carry_ref[:, c0:c0 + MXU_COLS] = u[tm - SUBLANES:, :]
        cw = cw_ref[:, c0:c0 + MXU_COLS]
        return (cb_ref[:, c0:c0 + MXU_COLS]
                + cw[0:1] * ext_ref[slot, SUBLANES - 2:SUBLANES - 2 + tm, :]
                + cw[1:2] * ext_ref[slot, SUBLANES - 1:SUBLANES - 1 + tm, :]
                + cw[2:3] * u)

    for c in range(D_FF // MXU_COLS):
        c0 = c * MXU_COLS
        g = conv_cols(0, c0)
        v = conv_cols(1, D_FF + c0)
        th = jnp.tanh(g * (g * g * (GELU_C * 0.044715) + GELU_C))
        a_ref[:, c0:c0 + MXU_COLS] = (g * v * (th + 1.0)).astype(BF16)


def _mix_up(o_da, o_rp, x, w_out, g_post, g_pre, w_up, cw, cb, seq, layer):
    n = x.shape[0]
    row = lambda width: pl.BlockSpec((TM_MIX, width), lambda i: (i, 0))

    def resident(arr):
        tail = arr.shape[1:]
        return pl.BlockSpec((None,) + tail, lambda *_: (layer,) + (0,) * len(tail),
                            pipeline_mode=pl.Buffered(1))

    return pl.pallas_call(
        functools.partial(_mix_up_kernel, tiles_per_seq=seq // TM_MIX),
        grid=(n // TM_MIX,),
        in_specs=[pl.BlockSpec((DA_WIDTH, TM_MIX), lambda i: (0, i)), row(RET_WIDTH + POOL_WIDTH), row(D_MODEL),
                  resident(w_out), _layer_spec(g_post, layer), _layer_spec(g_pre, layer),
                  resident(w_up), _layer_spec(cw, layer), _layer_spec(cb, layer)],
        out_specs=[row(D_MODEL), row(D_FF)],
        out_shape=[jax.ShapeDtypeStruct((n, D_MODEL), F32), jax.ShapeDtypeStruct((n, D_FF), BF16)],
        scratch_shapes=[
            pltpu.VMEM((SUBLANES, 2 * D_FF), F32),
            pltpu.VMEM((2, TM_MIX + SUBLANES, MXU_COLS), F32),
        ],
        compiler_params=_params("arbitrary"),
        name="mix_up",
    )(o_da, o_rp, x, w_out, g_post, g_pre, w_up, cw, cb)


def _down_proj_kernel(a_ref, x_ref, w_ref, g_ref, xo_ref):
    y = jnp.dot(a_ref[...], w_ref[...], preferred_element_type=F32)
    xo_ref[...] = x_ref[...] + _rms(y, g_ref[...])


def _down_proj(a, x, w, g, layer):
    n = x.shape[0]
    row = lambda width: pl.BlockSpec((TM_PROJ, width), lambda i: (i, 0))
    return pl.pallas_call(
        _down_proj_kernel,
        grid=(n // TM_PROJ,),
        in_specs=[row(D_FF), row(D_MODEL), _layer_spec(w, layer), _layer_spec(g, layer)],
        out_specs=row(D_MODEL),
        out_shape=jax.ShapeDtypeStruct((n, D_MODEL), F32),
        compiler_params=_params("arbitrary"),
        name="down_proj",
    )(a, x, w, g)


def _rotary_tables(seq, rot_dim, theta, period):
    inv = jnp.float32(theta) ** (-jnp.arange(0, rot_dim, 2, dtype=F32) / rot_dim)
    ang = jnp.arange(seq, dtype=F32)[:, None] * inv[None, :]
    cos, sin = jnp.cos(ang), jnp.sin(ang)
    half = rot_dim // 2
    zh = jnp.zeros((seq, half), F32)
    zp = jnp.zeros((seq, period - rot_dim), F32)
    one_period = jnp.stack([jnp.concatenate([cos, cos, jnp.ones_like(zp)], axis=1),
                            jnp.concatenate([zh, sin, zp], axis=1),
                            jnp.concatenate([-sin, zh, zp], axis=1)], axis=1)
    return jnp.broadcast_to(one_period[:, :, None, :],
                            (seq, 3, LANES // period, period)).reshape(seq, 3 * LANES)


def _retention_tables():
    H, C = RET_HEADS, RET_CHUNK
    f32 = np.float32
    log_g = np.log(f32(1.0) - f32(2.0) ** (f32(-5.0) - np.arange(H, dtype=f32)))
    idx = np.arange(C, dtype=f32)
    diff = idx[:, None] - idx[None, :]
    decay = np.where(diff >= 0, np.exp(np.maximum(diff, f32(0.0)) * log_g[:, None, None]), f32(0.0))
    xi = np.exp((idx + f32(1.0)) * log_g[:, None])
    zeta = np.exp((f32(C) - f32(1.0) - idx) * log_g[:, None])
    chunk_decay = np.exp(f32(C) * log_g)
    xi_t = np.repeat(xi.T, RET_QK_DIM, axis=1)
    zeta_t = np.repeat(zeta.T, RET_QK_DIM, axis=1)
    row_head = np.arange(LANES) // RET_QK_DIM
    col_head = np.arange(RET_WIDTH) // RET_V_DIM
    bd = (row_head[:, None] == col_head[None, :]).astype(f32)
    cd = bd * chunk_decay[row_head][:, None]
    avg = (col_head[:, None] == col_head[None, :]).astype(f32) / f32(RET_V_DIM)
    win = np.repeat(np.asarray(POOL_WINDOWS, f32), POOL_DIM)[None, :]
    tabs = [jnp.asarray(t.astype(f32)) for t in (decay, xi_t, zeta_t, cd, bd)]
    return (*tabs, jnp.asarray(avg, BF16), jnp.asarray(win))


def _block_diag(w):
    L, G, P, _ = w.shape
    eye = jnp.eye(G, dtype=w.dtype)
    return (eye[None, :, None, :, None] * w[:, :, :, None, :]).reshape(L, G * P, G * P)


def kernel(x, norm_mix_pre, norm_mix_post, w_in, lambda_q1, lambda_k1, lambda_q2, lambda_k2, diff_subln,
           ret_norm, pool_w, pool_scale, w_out, norm_mlp_pre, norm_mlp_post, w_up, conv_w, conv_b, w_down):
    batch, seq, _ = x.shape
    depth = w_in.shape[0]
    assert seq % TQ == 0 and seq % TM_PROJ == 0 and seq % TM_MIX == 0
    xf = x.reshape(batch * seq, D_MODEL)

    tda = _rotary_tables(seq, DA_ROT_DIM, ROPE_THETA, DA_QK_DIM)
    tret = _rotary_tables(seq, RET_QK_DIM, RET_THETA, RET_QK_DIM)
    rtabs = _retention_tables()

    rows = lambda a: a.reshape(depth, 1, -1)
    w_in_b, w_out_b, w_up_b, w_down_b = (w.astype(BF16) for w in (w_in, w_out, w_up, w_down))
    pool_w_b = _block_diag(pool_w).astype(BF16)
    half_val = jnp.concatenate([jnp.ones((D_FF,), F32), jnp.full((D_FF,), 0.5, F32)])
    conv_w_s, conv_b_s = conv_w * half_val, rows(conv_b * half_val)
    g_mix_pre, g_mix_post, g_mlp_pre, g_mlp_post = (rows(g) for g in (norm_mix_pre, norm_mix_post,
                                                                      norm_mlp_pre, norm_mlp_post))
    lq1, lk1, lq2, lk2 = (rows(v) for v in (lambda_q1, lambda_k1, lambda_q2, lambda_k2))
    g_sub = diff_subln.reshape(depth, DA_V_DIM, 1)
    g_ret, p_scale = rows(ret_norm), rows(pool_scale)

    for l in range(depth):
        lam_init = 0.8 - 0.6 * math.exp(-0.3 * l)
        proj, o_rp = _in_mix(xf, g_mix_pre, w_in_b, tda, tret, rtabs, g_ret, pool_w_b, p_scale, seq, l)
        o_da = _diff_attn(proj, lq1, lk1, lq2, lk2, g_sub, lam_init, batch, seq, l)
        xf, a = _mix_up(o_da, o_rp, xf, w_out_b, g_mix_post, g_mlp_pre, w_up_b, conv_w_s, conv_b_s, seq, l)
        xf = _down_proj(a, xf, w_down_b, g_mlp_post, l)
    return xf.reshape(batch, seq, D_MODEL)
```

```python
import functools
import math

import jax
import jax.numpy as jnp
import numpy as np
from jax import lax
from jax.experimental import pallas as pl
from jax.experimental.pallas import tpu as pltpu

F32 = jnp.float32
BF16 = jnp.bfloat16

D_MODEL = 1024
DA_HEADS = 4
DA_QK_DIM = 64
DA_V_DIM = 128
DA_ROT_DIM = 16
ROPE_THETA = 500000.0
RET_HEADS = 4
RET_QK_DIM = 32
RET_V_DIM = 64
RET_THETA = 10000.0
RET_CHUNK = 128
POOL_GROUPS = 4
POOL_DIM = 64
POOL_WINDOWS = (2, 4, 8, 16)
POOL_HALO = 16
DA_WIDTH = 512
RET_WIDTH = 256
POOL_WIDTH = 256
IN_WIDTH = 2560
D_FF = 2816
CONV_WIDTH = 3
EPS = 1e-6

COL_Q_DA, COL_K_DA, COL_V_DA = 0, 512, 1024
COL_QK_R, COL_V_R, COL_G_R, COL_U = 1536, 1792, 2048, 2304

LANES = 128
SUBLANES = 8
MXU_COLS = 256
VMEM_LIMIT = 56 * 1024 * 1024

TM_PROJ = 1024
TM_MIX = 512
TQ = 1024
TK = 512
SUB = 256
QK_AHEAD = 2
VT_ROWS = DA_V_DIM + 16
LOG2E = math.log2(math.e)
GELU_C = math.sqrt(2.0 / math.pi)


def _params(*sem):
    return pltpu.CompilerParams(dimension_semantics=sem, vmem_limit_bytes=VMEM_LIMIT)


def _layer_spec(arr, layer):
    tail = arr.shape[1:]
    return pl.BlockSpec((None,) + tail, lambda *_: (layer,) + (0,) * len(tail))


def _rms(x, g):
    return x * lax.rsqrt(jnp.mean(x * x, axis=-1, keepdims=True) + EPS) * g


def _rot_tables(tabs, tile):
    ta_ref, tb_ref, msk_ref = tabs
    ca, sa = ta_ref[0, pl.ds(tile, 1), :], ta_ref[1, pl.ds(tile, 1), :]
    cb, sb = tb_ref[0], tb_ref[1]
    s = sa * cb + ca * sb
    return ca * cb - sa * sb, s * msk_ref[0:1, :], s * msk_ref[1:2, :]


def _rot(x, tabs, shift):
    return x * tabs[0] + pltpu.roll(x, shift, 1) * tabs[1] + pltpu.roll(x, LANES - shift, 1) * tabs[2]


def _in_mix_kernel(x_ref, g_ref, w_ref, *rest, tiles_per_seq):
    rot_refs, ret_consts, (o_ref, orp_ref), ret_scratch = rest[:6], rest[6:16], rest[16:18], rest[18:]
    i = pl.program_id(0)
    h = _rms(x_ref[...], g_ref[...]).astype(BF16)
    tda = _rot_tables(rot_refs[:3], i % tiles_per_seq)
    tret = _rot_tables(rot_refs[3:], i % tiles_per_seq)
    kept = {}

    def column(c0):
        p = jnp.dot(h, w_ref[:, c0:c0 + MXU_COLS], preferred_element_type=F32)
        lo, hi = p[:, :LANES], p[:, LANES:]
        if c0 < COL_V_DA:
            lo = _rot(lo, tda, DA_ROT_DIM // 2)
            hi = _rot(hi, tda, DA_ROT_DIM // 2)
            if c0 < COL_K_DA:
                lo = lo * (DA_QK_DIM ** -0.5 * LOG2E)
                hi = hi * (DA_QK_DIM ** -0.5 * LOG2E)
        elif c0 == COL_QK_R:
            lo = _rot(lo, tret, RET_QK_DIM // 2)
            hi = _rot(hi, tret, RET_QK_DIM // 2) * (RET_QK_DIM ** -0.5)
        if c0 < COL_QK_R:
            o_ref[:, c0:c0 + LANES] = lo.astype(BF16)
            o_ref[:, c0 + LANES:c0 + MXU_COLS] = hi.astype(BF16)
        else:
            kept[c0] = jnp.concatenate([lo.astype(BF16), hi.astype(BF16)], axis=1)

    for c0 in range(COL_QK_R, IN_WIDTH, MXU_COLS):
        column(c0)
    fillers = [functools.partial(column, c0) for c0 in range(0, COL_QK_R, MXU_COLS)]
    _ret_pool_body(i % tiles_per_seq == 0, (i % tiles_per_seq) * x_ref.shape[0],
                   kept[COL_QK_R], kept[COL_V_R], kept[COL_G_R], kept[COL_U],
                   *ret_consts, orp_ref, *ret_scratch, fillers=fillers)
    for f in fillers:
        f()


def _in_mix(x, g, w, tda, tret, rtabs, ret_g, pool_w_bd, pool_scale, seq, layer):
    n = x.shape[0]
    tiles_per_seq = seq // TM_PROJ
    decay, xi, zeta, cd, bd, avg, win = rtabs
    const = lambda a: pl.BlockSpec(a.shape, lambda i: (0,) * a.ndim)
    row = lambda width: pl.BlockSpec((TM_PROJ, width), lambda i: (i, 0))
    return pl.pallas_call(
        functools.partial(_in_mix_kernel, tiles_per_seq=tiles_per_seq),
        grid=(n // TM_PROJ,),
        in_specs=[
            row(D_MODEL),
            _layer_spec(g, layer),
            _layer_spec(w, layer),
            *(const(t) for t in tda), *(const(t) for t in tret),
            const(decay), const(xi), const(zeta), const(cd), const(bd), const(avg),
            _layer_spec(ret_g, layer), const(win), _layer_spec(pool_w_bd, layer), _layer_spec(pool_scale, layer),
        ],
        out_specs=[row(COL_QK_R), row(RET_WIDTH + POOL_WIDTH)],
        out_shape=[jax.ShapeDtypeStruct((n, COL_QK_R), BF16),
                   jax.ShapeDtypeStruct((n, RET_WIDTH + POOL_WIDTH), BF16)],
        scratch_shapes=[
            pltpu.VMEM((LANES, RET_WIDTH), F32),
            pltpu.VMEM((POOL_HALO, POOL_WIDTH), F32),
            pltpu.VMEM((TM_PROJ, RET_WIDTH), F32),
        ],
        compiler_params=_params("arbitrary"),
        name="in_mix",
    )(x, g, w, *tda, *tret, decay, xi, zeta, cd, bd, avg, ret_g, win, pool_w_bd, pool_scale)


def _diff_attn_kernel(q_ref, qn_ref, k_ref, v_ref, lq1_ref, lk1_ref, lq2_ref, lk2_ref, g_ref, o_ref,
                      vt_ref, qt_ref, qtn_ref, s0_ref, s1_ref, mx0_ref, mx1_ref, m_ref, acc_ref, *, lam_init):
    qi = pl.program_id(2)
    nkv = v_ref.shape[0] // TK
    s_refs, mx_refs = (s0_ref, s1_ref), (mx0_ref, mx1_ref)

    @pl.when(qi == 0)
    def _():
        for c in range(nkv):
            vt_ref[c, :DA_V_DIM, :] = v_ref[c * TK:(c + 1) * TK, :].astype(F32).T.astype(BF16)
            vt_ref[c, DA_V_DIM:, :] = jnp.ones((VT_ROWS - DA_V_DIM, TK), BF16)

    def store_qt(dst_ref, src_ref):
        qt = src_ref[...].astype(F32).T
        feat = lax.broadcasted_iota(jnp.int32, qt.shape, 0)
        dst_ref[0] = jnp.where(feat < DA_QK_DIM, qt, 0.0).astype(BF16)
        dst_ref[1] = jnp.where(feat >= DA_QK_DIM, qt, 0.0).astype(BF16)

    m_ref[...] = jnp.full(m_ref.shape, -jnp.inf, F32)
    acc_ref[...] = jnp.zeros(acc_ref.shape, F32)

    def columns(diag):
        out = []
        for qb in range(TQ // SUB):
            if diag is not None and diag * TK > (qb + 1) * SUB - 1:
                continue
            mask = diag is not None and (diag + 1) * TK - 1 > qb * SUB
            out += [(2 * qb + mi, qb, mi, mask) for mi in range(2)]
        return out

    def score(c, slot, diag, q_src=qt_ref):
        def one(j, qb, mi, mask):
            ks = k_ref[pl.ds(pl.multiple_of(c * TK, TK), TK), :]
            st = jnp.dot(ks, q_src[mi, :, qb * SUB:(qb + 1) * SUB],
                         preferred_element_type=F32)
            if mask:
                key = diag * TK + lax.broadcasted_iota(jnp.int32, st.shape, 0)
                qry = qb * SUB + lax.broadcasted_iota(jnp.int32, st.shape, 1)
                st = jnp.where(key <= qry, st, -jnp.inf)
            s_refs[slot][j] = st
            mx_refs[slot][j] = jnp.max(st, axis=0, keepdims=True)
        return [functools.partial(one, *col) for col in columns(diag)]

    def update(c, slot, diag):
        def one(j, qb, mi, mask):
            qs = slice(qb * SUB, (qb + 1) * SUB)
            m_prev = m_ref[mi, :, qs]
            m_new = jnp.maximum(m_prev, mx_refs[slot][j])
            alpha = jnp.exp2(m_prev - m_new)
            p = jnp.exp2(s_refs[slot][j] - m_new).astype(BF16)
            acc_ref[mi, :, qs] = (acc_ref[mi, :, qs] * alpha
                                  + jnp.dot(vt_ref[c], p, preferred_element_type=F32))
            m_ref[mi, :, qs] = m_new
        return [functools.partial(one, *col) for col in columns(diag)]

    def emit(scores, updates):
        scores = list(scores)
        for s in scores[:QK_AHEAD]:
            s()
        rest = scores[QK_AHEAD:]
        for u in updates:
            u()
            if rest:
                rest.pop(0)()
        for s in rest:
            s()

    assert TQ // TK == 2
    c0 = 2 * qi

    @pl.when(qi > 0)
    def _():
        qt_ref[...] = qtn_ref[...]

        def body(i, carry):
            emit(score(2 * i + 1, 1, None), update(2 * i, 0, None))
            emit(score(2 * i + 2, 0, None), update(2 * i + 1, 1, None))
            return carry

        lax.fori_loop(0, qi - 1, body, 0)
        emit(score(c0 - 1, 1, None), update(c0 - 2, 0, None))
        emit(score(c0, 0, 0), update(c0 - 1, 1, None))

    @pl.when(qi == 0)
    def _():
        store_qt(qt_ref, q_ref)
        emit(score(c0, 0, 0), [])

    store_qt(qtn_ref, qn_ref)
    emit(score(c0 + 1, 1, 1), update(c0, 0, 0))
    last_tile = qi == pl.num_programs(2) - 1

    @pl.when(jnp.logical_not(last_tile))
    def _():
        emit(score(0, 0, None, qtn_ref), update(c0 + 1, 1, 1))

    @pl.when(last_tile)
    def _():
        emit([], update(c0 + 1, 1, 1))

    lam = (jnp.exp(jnp.sum(lq1_ref[...] * lk1_ref[...], axis=-1, keepdims=True))
           - jnp.exp(jnp.sum(lq2_ref[...] * lk2_ref[...], axis=-1, keepdims=True)) + lam_init)
    o1 = acc_ref[0, :DA_V_DIM, :] / acc_ref[0, DA_V_DIM:DA_V_DIM + 1, :]
    o2 = acc_ref[1, :DA_V_DIM, :] / acc_ref[1, DA_V_DIM:DA_V_DIM + 1, :]
    a = o1 - lam * o2
    y = a * lax.rsqrt(jnp.mean(a * a, axis=0, keepdims=True) + EPS) * g_ref[...]
    o_ref[...] = (y * (1.0 - lam_init)).astype(BF16)


def _diff_attn(proj, lq1, lk1, lq2, lk2, g, lam_init, batch, seq, layer):
    n = proj.shape[0]
    nq = seq // TQ
    kcol, vcol = COL_K_DA // LANES, COL_V_DA // LANES
    return pl.pallas_call(
        functools.partial(_diff_attn_kernel, lam_init=lam_init),
        grid=(batch, DA_HEADS, nq),
        in_specs=[
            pl.BlockSpec((TQ, LANES), lambda b, h, i: (b * nq + i, h)),
            pl.BlockSpec((TQ, LANES), lambda b, h, i: (b * nq + jnp.minimum(i + 1, nq - 1), h)),
            pl.BlockSpec((seq, LANES), lambda b, h, i: (b, kcol + h)),
            pl.BlockSpec((seq, LANES), lambda b, h, i: (b, vcol + h)),
            _layer_spec(lq1, layer), _layer_spec(lk1, layer), _layer_spec(lq2, layer), _layer_spec(lk2, layer),
            _layer_spec(g, layer),
        ],
        out_specs=pl.BlockSpec((LANES, TQ), lambda b, h, i: (h, b * nq + i)),
        out_shape=jax.ShapeDtypeStruct((DA_WIDTH, n), BF16),
        scratch_shapes=[
            pltpu.VMEM((seq // TK, VT_ROWS, TK), BF16),
            pltpu.VMEM((2, LANES, TQ), BF16),
            pltpu.VMEM((2, LANES, TQ), BF16),
            pltpu.VMEM((2 * TQ // SUB, TK, SUB), F32),
            pltpu.VMEM((2 * TQ // SUB, TK, SUB), F32),
            pltpu.VMEM((2 * TQ // SUB, 1, SUB), F32),
            pltpu.VMEM((2 * TQ // SUB, 1, SUB), F32),
            pltpu.VMEM((2, 1, TQ), F32),
            pltpu.VMEM((2, VT_ROWS, TQ), F32),
        ],
        compiler_params=_params("arbitrary", "arbitrary", "arbitrary"),
        name="diff_attn",
    )(proj, proj, proj, proj, lq1, lk1, lq2, lk2, g)


def _split_dot(x, w):
    hi = x.astype(BF16)
    lo = (x - hi.astype(F32)).astype(BF16)
    return (jnp.dot(hi, w, preferred_element_type=F32) + jnp.dot(lo, w, preferred_element_type=F32))


def _ret_pool_body(seq_start, t0, qk, v, g, u, decay_ref, xi_ref, zeta_ref, cd_ref, bd_ref,
                   avg_ref, retg_ref, win_ref, pw_ref, ps_ref, o_ref, state_ref, halo_ref, oret_ref, *, fillers):
    rows = qk.shape[0]

    def fill():
        if fillers:
            fillers.pop(0)()

    @pl.when(seq_start)
    def _():
        state_ref[...] = jnp.zeros(state_ref.shape, F32)
        halo_ref[...] = jnp.zeros(halo_ref.shape, F32)

    C = RET_CHUNK
    chunks = range(rows // C)
    qlane = lax.broadcasted_iota(jnp.int32, (C, LANES), 1) // RET_QK_DIM
    vlane = lax.broadcasted_iota(jnp.int32, (C, RET_WIDTH), 1) // RET_V_DIM
    qs = [qk[c * C:(c + 1) * C, :LANES] for c in chunks]
    ks = [qk[c * C:(c + 1) * C, LANES:] for c in chunks]
    vs = [v[c * C:(c + 1) * C, :] for c in chunks]
    inner = [[lax.dot_general(jnp.where(qlane == h, qs[c], jnp.zeros_like(qs[c])), ks[c],
                              (((1,), (1,)), ((), ())), preferred_element_type=F32)
              for h in range(RET_HEADS)] for c in chunks]
    upd = [lax.dot_general((ks[c].astype(F32) * zeta_ref[...]).astype(BF16), vs[c],
                           (((0,), (0,)), ((), ())), preferred_element_type=F32) for c in chunks]
    fill()
    states = [state_ref[...]]
    for c in chunks:
        states.append(states[c] * cd_ref[...] + bd_ref[...] * upd[c])
    state_ref[...] = states[-1]
    for c in chunks:
        o_c = jnp.dot((qs[c].astype(F32) * xi_ref[...]).astype(BF16), states[c].astype(BF16),
                      preferred_element_type=F32)
        for h in range(RET_HEADS):
            vh = jnp.where(vlane == h, vs[c], jnp.zeros_like(vs[c]))
            o_c = o_c + jnp.dot((inner[c][h] * decay_ref[h]).astype(BF16), vh, preferred_element_type=F32)
        oret_ref[c * C:(c + 1) * C, :] = o_c
        if c % 2 == 1:
            fill()

    o = oret_ref[...]
    mu = _split_dot(o, avg_ref[...])
    d = o - mu
    var = jnp.dot((d * d).astype(BF16), avg_ref[...], preferred_element_type=F32)
    y = d * lax.rsqrt(var + EPS) * retg_ref[...]
    gate = g.astype(F32)
    o_ref[:, :RET_WIDTH] = (gate * jax.nn.sigmoid(gate) * y).astype(BF16)
    fill()

    u = u.astype(F32)
    halo_groups = POOL_HALO // SUBLANES
    ext = jnp.concatenate([halo_ref[...], u], axis=0).reshape(-1, SUBLANES, POOL_WIDTH)
    halo_ref[...] = u[rows - POOL_HALO:, :]

    def shift_down(x, k):
        padded = jnp.concatenate([jnp.zeros_like(x[:1]), x], axis=0)
        if k == SUBLANES:
            return padded[:-1]
        r = pltpu.roll(padded, k, 1)
        return jnp.where(lax.broadcasted_iota(jnp.int32, x.shape, 1) < k, r[:-1], r[1:])

    assert POOL_WINDOWS == (2, 4, 8, 16) and POOL_DIM * 2 == LANES
    s2 = ext + shift_down(ext, 1)
    s4 = s2 + shift_down(s2, 2)
    s4_hi = s4[:, :, LANES:]
    s8 = s4_hi + shift_down(s4_hi, 4)
    s16 = s8 + shift_down(s8, 8)
    first = lax.broadcasted_iota(jnp.int32, s8.shape, 2) < POOL_DIM
    psum = jnp.concatenate([jnp.where(first, s2[:, :, :LANES], s4[:, :, :LANES]), jnp.where(first, s8, s16)],
                           axis=2)[halo_groups:].reshape(rows, POOL_WIDTH)
    t = (t0 + lax.broadcasted_iota(jnp.int32, u.shape, 0)).astype(F32)
    pooled = psum / jnp.minimum(t + 1.0, win_ref[...]) - u
    yp = jnp.dot(pooled.astype(BF16), pw_ref[...], preferred_element_type=F32) * ps_ref[...]
    o_ref[:, RET_WIDTH:] = yp.astype(BF16)


def _mix_up_kernel(oda_ref, orp_ref, x_ref, wo_ref, gpost_ref, gpre_ref, w_ref, cw_ref, cb_ref,
                   xo_ref, a_ref, carry_ref, ext_ref, *, tiles_per_seq):
    i = pl.program_id(0)
    tm = x_ref.shape[0]

    @pl.when(i % tiles_per_seq == 0)
    def _():
        carry_ref[...] = jnp.zeros(carry_ref.shape, F32)

    mix = (lax.dot_general(oda_ref[...], wo_ref[:DA_WIDTH, :], (((0,), (0,)), ((), ())),
                           preferred_element_type=F32)
           + jnp.dot(orp_ref[...], wo_ref[DA_WIDTH:, :], preferred_element_type=F32))
    xn = x_ref[...] + _rms(mix, gpost_ref[...])
    xo_ref[...] = xn
    h = _rms(xn, gpre_ref[...]).astype(BF16)

    def conv_cols(slot, c0):
        u = jnp.dot(h, w_ref[:, c0:c0 + MXU_COLS], preferred_element_type=F32)
        ext_ref[slot, :SUBLANES, :] = carry_ref[:, c0:c0 + MXU_COLS]
        ext_ref[slot, SUBLANES:, :] = u
        carry_ref[:, c0:c0 + MXU_COLS] = u[tm - SUBLANES:, :]
        cw = cw_ref[:, c0:c0 + MXU_COLS]
        return (cb_ref[:, c0:c0 + MXU_COLS]
                + cw[0:1] * ext_ref[slot, SUBLANES - 2:SUBLANES - 2 + tm, :]
                + cw[1:2] * ext_ref[slot, SUBLANES - 1:SUBLANES - 1 + tm, :]
                + cw[2:3] * u)

    for c in range(D_FF // MXU_COLS):
        c0 = c * MXU_COLS
        g = conv_cols(0, c0)
        v = conv_cols(1, D_FF + c0)
        th = jnp.tanh(g * (g * g * (GELU_C * 0.044715) + GELU_C))
        a_ref[:, c0:c0 + MXU_COLS] = (g * v * (th + 1.0)).astype(BF16)


def _mix_up(o_da, o_rp, x, w_out, g_post, g_pre, w_up, cw, cb, seq, layer):
    n = x.shape[0]
    row = lambda width: pl.BlockSpec((TM_MIX, width), lambda i: (i, 0))

    def resident(arr):
        tail = arr.shape[1:]
        return pl.BlockSpec((None,) + tail, lambda *_: (layer,) + (0,) * len(tail),
                            pipeline_mode=pl.Buffered(1))

    return pl.pallas_call(
        functools.partial(_mix_up_kernel, tiles_per_seq=seq // TM_MIX),
        grid=(n // TM_MIX,),
        in_specs=[pl.BlockSpec((DA_WIDTH, TM_MIX), lambda i: (0, i)), row(RET_WIDTH + POOL_WIDTH), row(D_MODEL),
                  resident(w_out), _layer_spec(g_post, layer), _layer_spec(g_pre, layer),
                  resident(w_up), _layer_spec(cw, layer), _layer_spec(cb, layer)],
        out_specs=[row(D_MODEL), row(D_FF)],
        out_shape=[jax.ShapeDtypeStruct((n, D_MODEL), F32), jax.ShapeDtypeStruct((n, D_FF), BF16)],
        scratch_shapes=[
            pltpu.VMEM((SUBLANES, 2 * D_FF), F32),
            pltpu.VMEM((2, TM_MIX + SUBLANES, MXU_COLS), F32),
        ],
        compiler_params=_params("arbitrary"),
        name="mix_up",
    )(o_da, o_rp, x, w_out, g_post, g_pre, w_up, cw, cb)


def _down_proj_kernel(a_ref, x_ref, w_ref, g_ref, xo_ref):
    y = jnp.dot(a_ref[...], w_ref[...], preferred_element_type=F32)
    xo_ref[...] = x_ref[...] + _rms(y, g_ref[...])


def _down_proj(a, x, w, g, layer):
    n = x.shape[0]
    row = lambda width: pl.BlockSpec((TM_PROJ, width), lambda i: (i, 0))
    return pl.pallas_call(
        _down_proj_kernel,
        grid=(n // TM_PROJ,),
        in_specs=[row(D_FF), row(D_MODEL), _layer_spec(w, layer), _layer_spec(g, layer)],
        out_specs=row(D_MODEL),
        out_shape=jax.ShapeDtypeStruct((n, D_MODEL), F32),
        compiler_params=_params("arbitrary"),
        name="down_proj",
    )(a, x, w, g)


def _rotary_tables(seq, rot_dim, theta, period, tile):
    half = rot_dim // 2
    d = np.arange(LANES) % period
    rotated = d < rot_dim
    inv = jnp.float32(theta) ** (-jnp.arange(0, rot_dim, 2, dtype=F32) / rot_dim)
    inv_lane = jnp.where(jnp.asarray(rotated), inv[np.where(rotated, d % half, 0)], 0.0)
    start = (jnp.arange(seq // tile, dtype=F32) * tile)[:, None] * inv_lane[None, :]
    row = jnp.arange(tile, dtype=F32)[:, None] * inv_lane[None, :]
    masks = np.stack([(rotated & (d >= half)).astype(np.float32), -(rotated & (d < half)).astype(np.float32)])
    return (jnp.stack([jnp.cos(start), jnp.sin(start)]), jnp.stack([jnp.cos(row), jnp.sin(row)]),
            jnp.asarray(masks))


def _retention_tables():
    H, C = RET_HEADS, RET_CHUNK
    f32 = np.float32
    log_g = np.log(f32(1.0) - f32(2.0) ** (f32(-5.0) - np.arange(H, dtype=f32)))
    idx = np.arange(C, dtype=f32)
    diff = idx[:, None] - idx[None, :]
    decay = np.where(diff >= 0, np.exp(np.maximum(diff, f32(0.0)) * log_g[:, None, None]), f32(0.0))
    xi = np.exp((idx + f32(1.0)) * log_g[:, None])
    zeta = np.exp((f32(C) - f32(1.0) - idx) * log_g[:, None])
    chunk_decay = np.exp(f32(C) * log_g)
    xi_t = np.repeat(xi.T, RET_QK_DIM, axis=1)
    zeta_t = np.repeat(zeta.T, RET_QK_DIM, axis=1)
    row_head = np.arange(LANES) // RET_QK_DIM
    col_head = np.arange(RET_WIDTH) // RET_V_DIM
    bd = (row_head[:, None] == col_head[None, :]).astype(f32)
    cd = bd * chunk_decay[row_head][:, None]
    avg = (col_head[:, None] == col_head[None, :]).astype(f32) / f32(RET_V_DIM)
    win = np.repeat(np.asarray(POOL_WINDOWS, f32), POOL_DIM)[None, :]
    tabs = [jnp.asarray(t.astype(f32)) for t in (decay, xi_t, zeta_t, cd, bd)]
    return (*tabs, jnp.asarray(avg, BF16), jnp.asarray(win))


def _block_diag(w):
    L, G, P, _ = w.shape
    eye = jnp.eye(G, dtype=w.dtype)
    return (eye[None, :, None, :, None] * w[:, :, :, None, :]).reshape(L, G * P, G * P)


def kernel(x, norm_mix_pre, norm_mix_post, w_in, lambda_q1, lambda_k1, lambda_q2, lambda_k2, diff_subln,
           ret_norm, pool_w, pool_scale, w_out, norm_mlp_pre, norm_mlp_post, w_up, conv_w, conv_b, w_down):
    batch, seq, _ = x.shape
    depth = w_in.shape[0]
    assert seq % TQ == 0 and seq % TM_PROJ == 0 and seq % TM_MIX == 0
    xf = x.reshape(batch * seq, D_MODEL)

    tda = _rotary_tables(seq, DA_ROT_DIM, ROPE_THETA, DA_QK_DIM, TM_PROJ)
    tret = _rotary_tables(seq, RET_QK_DIM, RET_THETA, RET_QK_DIM, TM_PROJ)
    rtabs = _retention_tables()

    rows = lambda a: a.reshape(depth, 1, -1)
    w_in_b, w_out_b, w_up_b, w_down_b = (w.astype(BF16) for w in (w_in, w_out, w_up, w_down))
    pool_w_b = _block_diag(pool_w).astype(BF16)
    half_val = jnp.concatenate([jnp.ones((D_FF,), F32), jnp.full((D_FF,), 0.5, F32)])
    conv_w_s, conv_b_s = conv_w * half_val, rows(conv_b * half_val)
    g_mix_pre, g_mix_post, g_mlp_pre, g_mlp_post = (rows(g) for g in (norm_mix_pre, norm_mix_post,
                                                                      norm_mlp_pre, norm_mlp_post))
    lq1, lk1, lq2, lk2 = (rows(v) for v in (lambda_q1, lambda_k1, lambda_q2, lambda_k2))
    g_sub = diff_subln.reshape(depth, DA_V_DIM, 1)
    g_ret, p_scale = rows(ret_norm), rows(pool_scale)

    for l in range(depth):
        lam_init = 0.8 - 0.6 * math.exp(-0.3 * l)
        proj, o_rp = _in_mix(xf, g_mix_pre, w_in_b, tda, tret, rtabs, g_ret, pool_w_b, p_scale, seq, l)
        o_da = _diff_attn(proj, lq1, lk1, lq2, lk2, g_sub, lam_init, batch, seq, l)
        xf, a = _mix_up(o_da, o_rp, xf, w_out_b, g_mix_post, g_mlp_pre, w_up_b, conv_w_s, conv_b_s, seq, l)
        xf = _down_proj(a, xf, w_down_b, g_mlp_post, l)
    return xf.reshape(batch, seq, D_MODEL)
```

```python
import functools
import math

import jax
import jax.numpy as jnp
import numpy as np
from jax import lax
from jax.experimental import pallas as pl
from jax.experimental.pallas import tpu as pltpu

F32 = jnp.float32
BF16 = jnp.bfloat16

D_MODEL = 1024
DA_HEADS = 4
DA_QK_DIM = 64
DA_V_DIM = 128
DA_ROT_DIM = 16
ROPE_THETA = 500000.0
RET_HEADS = 4
RET_QK_DIM = 32
RET_V_DIM = 64
RET_THETA = 10000.0
RET_CHUNK = 128
POOL_GROUPS = 4
POOL_DIM = 64
POOL_WINDOWS = (2, 4, 8, 16)
POOL_HALO = 16
DA_WIDTH = 512
RET_WIDTH = 256
POOL_WIDTH = 256
IN_WIDTH = 2560
D_FF = 2816
CONV_WIDTH = 3
EPS = 1e-6

COL_Q_DA, COL_K_DA, COL_V_DA = 0, 512, 1024
COL_QK_R, COL_V_R, COL_G_R, COL_U = 1536, 1792, 2048, 2304

LANES = 128
SUBLANES = 8
MXU_COLS = 256
VMEM_LIMIT = 56 * 1024 * 1024

TM_PROJ = 1024
TM_MIX = 512
TQ = 1024
TK = 512
SUB = 256
QK_AHEAD = 2
VT_ROWS = DA_V_DIM + 16
LOG2E = math.log2(math.e)
GELU_C = math.sqrt(2.0 / math.pi)


def _params(*sem):
    return pltpu.CompilerParams(dimension_semantics=sem, vmem_limit_bytes=VMEM_LIMIT)


def _layer_spec(arr, layer):
    tail = arr.shape[1:]
    return pl.BlockSpec((None,) + tail, lambda *_: (layer,) + (0,) * len(tail))


def _resident_spec(arr, layer):
    tail = arr.shape[1:]
    return pl.BlockSpec((None,) + tail, lambda *_: (layer,) + (0,) * len(tail), pipeline_mode=pl.Buffered(1))


def _rms(x, g):
    return x * lax.rsqrt(jnp.mean(x * x, axis=-1, keepdims=True) + EPS) * g


def _rot_tables(tabs, tile):
    ta_ref, tb_ref, msk_ref = tabs
    ca, sa = ta_ref[0, pl.ds(tile, 1), :], ta_ref[1, pl.ds(tile, 1), :]
    cb, sb = tb_ref[0], tb_ref[1]
    s = sa * cb + ca * sb
    return ca * cb - sa * sb, s * msk_ref[0:1, :], s * msk_ref[1:2, :]


def _rot(x, tabs, shift):
    return x * tabs[0] + pltpu.roll(x, shift, 1) * tabs[1] + pltpu.roll(x, LANES - shift, 1) * tabs[2]


def _in_mix_kernel(x_ref, g_ref, w_ref, *rest, tiles_per_seq):
    rot_refs, ret_consts, (o_ref, orp_ref), ret_scratch, wb_ref = (rest[:6], rest[6:16], rest[16:18],
                                                                   rest[18:21], rest[21])
    i = pl.program_id(0)

    @pl.when(i == 0)
    def _():
        wb_ref[...] = w_ref[...].astype(BF16)

    h = _rms(x_ref[...], g_ref[...]).astype(BF16)
    tda = _rot_tables(rot_refs[:3], i % tiles_per_seq)
    tret = _rot_tables(rot_refs[3:], i % tiles_per_seq)
    kept = {}

    def column(c0):
        p = jnp.dot(h, wb_ref[:, c0:c0 + MXU_COLS], preferred_element_type=F32)
        lo, hi = p[:, :LANES], p[:, LANES:]
        if c0 < COL_V_DA:
            lo = _rot(lo, tda, DA_ROT_DIM // 2)
            hi = _rot(hi, tda, DA_ROT_DIM // 2)
            if c0 < COL_K_DA:
                lo = lo * (DA_QK_DIM ** -0.5 * LOG2E)
                hi = hi * (DA_QK_DIM ** -0.5 * LOG2E)
        elif c0 == COL_QK_R:
            lo = _rot(lo, tret, RET_QK_DIM // 2)
            hi = _rot(hi, tret, RET_QK_DIM // 2) * (RET_QK_DIM ** -0.5)
        if c0 < COL_QK_R:
            o_ref[:, c0:c0 + LANES] = lo.astype(BF16)
            o_ref[:, c0 + LANES:c0 + MXU_COLS] = hi.astype(BF16)
        else:
            kept[c0] = jnp.concatenate([lo.astype(BF16), hi.astype(BF16)], axis=1)

    for c0 in range(COL_QK_R, IN_WIDTH, MXU_COLS):
        column(c0)
    fillers = [functools.partial(column, c0) for c0 in range(0, COL_QK_R, MXU_COLS)]
    _ret_pool_body(i % tiles_per_seq == 0, (i % tiles_per_seq) * x_ref.shape[0],
                   kept[COL_QK_R], kept[COL_V_R], kept[COL_G_R], kept[COL_U],
                   *ret_consts, orp_ref, *ret_scratch, fillers=fillers)
    for f in fillers:
        f()


def _in_mix(x, g, w, tda, tret, rtabs, ret_g, pool_w_bd, pool_scale, seq, layer):
    n = x.shape[0]
    tiles_per_seq = seq // TM_PROJ
    decay, xi, zeta, cd, bd, avg, win = rtabs
    const = lambda a: pl.BlockSpec(a.shape, lambda i: (0,) * a.ndim)
    row = lambda width: pl.BlockSpec((TM_PROJ, width), lambda i: (i, 0))
    return pl.pallas_call(
        functools.partial(_in_mix_kernel, tiles_per_seq=tiles_per_seq),
        grid=(n // TM_PROJ,),
        in_specs=[
            row(D_MODEL),
            _layer_spec(g, layer),
            _resident_spec(w, layer),
            *(const(t) for t in tda), *(const(t) for t in tret),
            const(decay), const(xi), const(zeta), const(cd), const(bd), const(avg),
            _layer_spec(ret_g, layer), const(win), _layer_spec(pool_w_bd, layer), _layer_spec(pool_scale, layer),
        ],
        out_specs=[row(COL_QK_R), row(RET_WIDTH + POOL_WIDTH)],
        out_shape=[jax.ShapeDtypeStruct((n, COL_QK_R), BF16),
                   jax.ShapeDtypeStruct((n, RET_WIDTH + POOL_WIDTH), BF16)],
        scratch_shapes=[
            pltpu.VMEM((LANES, RET_WIDTH), F32),
            pltpu.VMEM((POOL_HALO, POOL_WIDTH), F32),
            pltpu.VMEM((TM_PROJ, RET_WIDTH), F32),
            pltpu.VMEM((D_MODEL, IN_WIDTH), BF16),
        ],
        compiler_params=_params("arbitrary"),
        name="in_mix",
    )(x, g, w, *tda, *tret, decay, xi, zeta, cd, bd, avg, ret_g, win, pool_w_bd, pool_scale)


def _diff_attn_kernel(q_ref, qn_ref, k_ref, v_ref, lq1_ref, lk1_ref, lq2_ref, lk2_ref, g_ref, o_ref,
                      vt_ref, qt_ref, qtn_ref, s0_ref, s1_ref, mx0_ref, mx1_ref, m_ref, acc_ref, *, lam_init):
    qi = pl.program_id(2)
    nkv = v_ref.shape[0] // TK
    s_refs, mx_refs = (s0_ref, s1_ref), (mx0_ref, mx1_ref)

    @pl.when(qi == 0)
    def _():
        for c in range(nkv):
            vt_ref[c, :DA_V_DIM, :] = v_ref[c * TK:(c + 1) * TK, :].astype(F32).T.astype(BF16)
            vt_ref[c, DA_V_DIM:, :] = jnp.ones((VT_ROWS - DA_V_DIM, TK), BF16)

    def store_qt(dst_ref, src_ref):
        qt = src_ref[...].astype(F32).T
        feat = lax.broadcasted_iota(jnp.int32, qt.shape, 0)
        dst_ref[0] = jnp.where(feat < DA_QK_DIM, qt, 0.0).astype(BF16)
        dst_ref[1] = jnp.where(feat >= DA_QK_DIM, qt, 0.0).astype(BF16)

    m_ref[...] = jnp.full(m_ref.shape, -jnp.inf, F32)
    acc_ref[...] = jnp.zeros(acc_ref.shape, F32)

    def columns(diag):
        out = []
        for qb in range(TQ // SUB):
            if diag is not None and diag * TK > (qb + 1) * SUB - 1:
                continue
            mask = diag is not None and (diag + 1) * TK - 1 > qb * SUB
            out += [(2 * qb + mi, qb, mi, mask) for mi in range(2)]
        return out

    def score(c, slot, diag, q_src=qt_ref):
        def one(j, qb, mi, mask):
            ks = k_ref[pl.ds(pl.multiple_of(c * TK, TK), TK), :]
            st = jnp.dot(ks, q_src[mi, :, qb * SUB:(qb + 1) * SUB],
                         preferred_element_type=F32)
            if mask:
                key = diag * TK + lax.broadcasted_iota(jnp.int32, st.shape, 0)
                qry = qb * SUB + lax.broadcasted_iota(jnp.int32, st.shape, 1)
                st = jnp.where(key <= qry, st, -jnp.inf)
            s_refs[slot][j] = st
            mx_refs[slot][j] = jnp.max(st, axis=0, keepdims=True)
        return [functools.partial(one, *col) for col in columns(diag)]

    def update(c, slot, diag):
        def one(j, qb, mi, mask):
            qs = slice(qb * SUB, (qb + 1) * SUB)
            m_prev = m_ref[mi, :, qs]
            m_new = jnp.maximum(m_prev, mx_refs[slot][j])
            alpha = jnp.exp2(m_prev - m_new)
            p = jnp.exp2(s_refs[slot][j] - m_new).astype(BF16)
            acc_ref[mi, :, qs] = (acc_ref[mi, :, qs] * alpha
                                  + jnp.dot(vt_ref[c], p, preferred_element_type=F32))
            m_ref[mi, :, qs] = m_new
        return [functools.partial(one, *col) for col in columns(diag)]

    def emit(scores, updates):
        scores = list(scores)
        for s in scores[:QK_AHEAD]:
            s()
        rest = scores[QK_AHEAD:]
        for u in updates:
            u()
            if rest:
                rest.pop(0)()
        for s in rest:
            s()

    assert TQ // TK == 2
    c0 = 2 * qi

    @pl.when(qi > 0)
    def _():
        qt_ref[...] = qtn_ref[...]

        def body(i, carry):
            emit(score(2 * i + 1, 1, None), update(2 * i, 0, None))
            emit(score(2 * i + 2, 0, None), update(2 * i + 1, 1, None))
            return carry

        lax.fori_loop(0, qi - 1, body, 0)
        emit(score(c0 - 1, 1, None), update(c0 - 2, 0, None))
        emit(score(c0, 0, 0), update(c0 - 1, 1, None))

    @pl.when(qi == 0)
    def _():
        store_qt(qt_ref, q_ref)
        emit(score(c0, 0, 0), [])

    store_qt(qtn_ref, qn_ref)
    emit(score(c0 + 1, 1, 1), update(c0, 0, 0))
    last_tile = qi == pl.num_programs(2) - 1

    @pl.when(jnp.logical_not(last_tile))
    def _():
        emit(score(0, 0, None, qtn_ref), update(c0 + 1, 1, 1))

    @pl.when(last_tile)
    def _():
        emit([], update(c0 + 1, 1, 1))

    lam = (jnp.exp(jnp.sum(lq1_ref[...] * lk1_ref[...], axis=-1, keepdims=True))
           - jnp.exp(jnp.sum(lq2_ref[...] * lk2_ref[...], axis=-1, keepdims=True)) + lam_init)
    o1 = acc_ref[0, :DA_V_DIM, :] / acc_ref[0, DA_V_DIM:DA_V_DIM + 1, :]
    o2 = acc_ref[1, :DA_V_DIM, :] / acc_ref[1, DA_V_DIM:DA_V_DIM + 1, :]
    a = o1 - lam * o2
    y = a * lax.rsqrt(jnp.mean(a * a, axis=0, keepdims=True) + EPS) * g_ref[...]
    o_ref[...] = (y * (1.0 - lam_init)).astype(BF16)


def _diff_attn(proj, lq1, lk1, lq2, lk2, g, lam_init, batch, seq, layer):
    n = proj.shape[0]
    nq = seq // TQ
    kcol, vcol = COL_K_DA // LANES, COL_V_DA // LANES
    return pl.pallas_call(
        functools.partial(_diff_attn_kernel, lam_init=lam_init),
        grid=(batch, DA_HEADS, nq),
        in_specs=[
            pl.BlockSpec((TQ, LANES), lambda b, h, i: (b * nq + i, h)),
            pl.BlockSpec((TQ, LANES), lambda b, h, i: (b * nq + jnp.minimum(i + 1, nq - 1), h)),
            pl.BlockSpec((seq, LANES), lambda b, h, i: (b, kcol + h)),
            pl.BlockSpec((seq, LANES), lambda b, h, i: (b, vcol + h)),
            _layer_spec(lq1, layer), _layer_spec(lk1, layer), _layer_spec(lq2, layer), _layer_spec(lk2, layer),
            _layer_spec(g, layer),
        ],
        out_specs=pl.BlockSpec((LANES, TQ), lambda b, h, i: (h, b * nq + i)),
        out_shape=jax.ShapeDtypeStruct((DA_WIDTH, n), BF16),
        scratch_shapes=[
            pltpu.VMEM((seq // TK, VT_ROWS, TK), BF16),
            pltpu.VMEM((2, LANES, TQ), BF16),
            pltpu.VMEM((2, LANES, TQ), BF16),
            pltpu.VMEM((2 * TQ // SUB, TK, SUB), F32),
            pltpu.VMEM((2 * TQ // SUB, TK, SUB), F32),
            pltpu.VMEM((2 * TQ // SUB, 1, SUB), F32),
            pltpu.VMEM((2 * TQ // SUB, 1, SUB), F32),
            pltpu.VMEM((2, 1, TQ), F32),
            pltpu.VMEM((2, VT_ROWS, TQ), F32),
        ],
        compiler_params=_params("arbitrary", "arbitrary", "arbitrary"),
        name="diff_attn",
    )(proj, proj, proj, proj, lq1, lk1, lq2, lk2, g)


def _split_dot(x, w):
    hi = x.astype(BF16)
    lo = (x - hi.astype(F32)).astype(BF16)
    return (jnp.dot(hi, w, preferred_element_type=F32) + jnp.dot(lo, w, preferred_element_type=F32))


def _ret_pool_body(seq_start, t0, qk, v, g, u, decay_ref, xi_ref, zeta_ref, cd_ref, bd_ref,
                   avg_ref, retg_ref, win_ref, pw_ref, ps_ref, o_ref, state_ref, halo_ref, oret_ref, *, fillers):
    rows = qk.shape[0]

    def fill():
        if fillers:
            fillers.pop(0)()

    @pl.when(seq_start)
    def _():
        state_ref[...] = jnp.zeros(state_ref.shape, F32)
        halo_ref[...] = jnp.zeros(halo_ref.shape, F32)

    C = RET_CHUNK
    chunks = range(rows // C)
    qlane = lax.broadcasted_iota(jnp.int32, (C, LANES), 1) // RET_QK_DIM
    vlane = lax.broadcasted_iota(jnp.int32, (C, RET_WIDTH), 1) // RET_V_DIM
    qs = [qk[c * C:(c + 1) * C, :LANES] for c in chunks]
    ks = [qk[c * C:(c + 1) * C, LANES:] for c in chunks]
    vs = [v[c * C:(c + 1) * C, :] for c in chunks]
    inner = [[lax.dot_general(jnp.where(qlane == h, qs[c], jnp.zeros_like(qs[c])), ks[c],
                              (((1,), (1,)), ((), ())), preferred_element_type=F32)
              for h in range(RET_HEADS)] for c in chunks]
    upd = [lax.dot_general((ks[c].astype(F32) * zeta_ref[...]).astype(BF16), vs[c],
                           (((0,), (0,)), ((), ())), preferred_element_type=F32) for c in chunks]
    fill()
    states = [state_ref[...]]
    for c in chunks:
        states.append(states[c] * cd_ref[...] + bd_ref[...] * upd[c])
    state_ref[...] = states[-1]
    for c in chunks:
        o_c = jnp.dot((qs[c].astype(F32) * xi_ref[...]).astype(BF16), states[c].astype(BF16),
                      preferred_element_type=F32)
        for h in range(RET_HEADS):
            vh = jnp.where(vlane == h, vs[c], jnp.zeros_like(vs[c]))
            o_c = o_c + jnp.dot((inner[c][h] * decay_ref[h]).astype(BF16), vh, preferred_element_type=F32)
        oret_ref[c * C:(c + 1) * C, :] = o_c
        if c % 2 == 1:
            fill()

    o = oret_ref[...]
    mu = _split_dot(o, avg_ref[...])
    d = o - mu
    var = jnp.dot((d * d).astype(BF16), avg_ref[...], preferred_element_type=F32)
    y = d * lax.rsqrt(var + EPS) * retg_ref[...]
    gate = g.astype(F32)
    o_ref[:, :RET_WIDTH] = (gate * jax.nn.sigmoid(gate) * y).astype(BF16)
    fill()

    u = u.astype(F32)
    halo_groups = POOL_HALO // SUBLANES
    ext = jnp.concatenate([halo_ref[...], u], axis=0).reshape(-1, SUBLANES, POOL_WIDTH)
    halo_ref[...] = u[rows - POOL_HALO:, :]

    def shift_down(x, k):
        padded = jnp.concatenate([jnp.zeros_like(x[:1]), x], axis=0)
        if k == SUBLANES:
            return padded[:-1]
        r = pltpu.roll(padded, k, 1)
        return jnp.where(lax.broadcasted_iota(jnp.int32, x.shape, 1) < k, r[:-1], r[1:])

    assert POOL_WINDOWS == (2, 4, 8, 16) and POOL_DIM * 2 == LANES
    s2 = ext + shift_down(ext, 1)
    s4 = s2 + shift_down(s2, 2)
    s4_hi = s4[:, :, LANES:]
    s8 = s4_hi + shift_down(s4_hi, 4)
    s16 = s8 + shift_down(s8, 8)
    first = lax.broadcasted_iota(jnp.int32, s8.shape, 2) < POOL_DIM
    psum = jnp.concatenate([jnp.where(first, s2[:, :, :LANES], s4[:, :, :LANES]), jnp.where(first, s8, s16)],
                           axis=2)[halo_groups:].reshape(rows, POOL_WIDTH)
    t = (t0 + lax.broadcasted_iota(jnp.int32, u.shape, 0)).astype(F32)
    pooled = psum / jnp.minimum(t + 1.0, win_ref[...]) - u
    yp = jnp.dot(pooled.astype(BF16), pw_ref[...], preferred_element_type=F32) * ps_ref[...]
    o_ref[:, RET_WIDTH:] = yp.astype(BF16)


def _mix_up_kernel(oda_ref, orp_ref, x_ref, wo_ref, gpost_ref, gpre_ref, w_ref, cw_ref, cb_ref,
                   xo_ref, a_ref, carry_ref, ext_ref, *, tiles_per_seq):
    i = pl.program_id(0)
    tm = x_ref.shape[0]

    @pl.when(i % tiles_per_seq == 0)
    def _():
        carry_ref[...] = jnp.zeros(carry_ref.shape, F32)

    mix = (lax.dot_general(oda_ref[...], wo_ref[:DA_WIDTH, :], (((0,), (0,)), ((), ())),
                           preferred_element_type=F32)
           + jnp.dot(orp_ref[...], wo_ref[DA_WIDTH:, :], preferred_element_type=F32))
    xn = x_ref[...] + _rms(mix, gpost_ref[...])
    xo_ref[...] = xn
    h = _rms(xn, gpre_ref[...]).astype(BF16)

    def conv_cols(slot, c0):
        u = jnp.dot(h, w_ref[:, c0:c0 + MXU_COLS], preferred_element_type=F32)
        ext_ref[slot, :SUBLANES, :] = carry_ref[:, c0:c0 + MXU_COLS]
        ext_ref[slot, SUBLANES:, :] = u
        carry_ref[:, c0:c0 + MXU_COLS] = u[tm - SUBLANES:, :]
        cw = cw_ref[:, c0:c0 + MXU_COLS]
        return (cb_ref[:, c0:c0 + MXU_COLS]
                + cw[0:1] * ext_ref[slot, SUBLANES - 2:SUBLANES - 2 + tm, :]
                + cw[1:2] * ext_ref[slot, SUBLANES - 1:SUBLANES - 1 + tm, :]
                + cw[2:3] * u)

    for c in range(D_FF // MXU_COLS):
        c0 = c * MXU_COLS
        g = conv_cols(0, c0)
        v = conv_cols(1, D_FF + c0)
        th = jnp.tanh(g * (g * g * (GELU_C * 0.044715) + GELU_C))
        a_ref[:, c0:c0 + MXU_COLS] = (g * v * (th + 1.0)).astype(BF16)


def _mix_up(o_da, o_rp, x, w_out, g_post, g_pre, w_up, cw, cb, seq, layer):
    n = x.shape[0]
    row = lambda width: pl.BlockSpec((TM_MIX, width), lambda i: (i, 0))
    resident = functools.partial(_resident_spec, layer=layer)
    return pl.pallas_call(
        functools.partial(_mix_up_kernel, tiles_per_seq=seq // TM_MIX),
        grid=(n // TM_MIX,),
        in_specs=[pl.BlockSpec((DA_WIDTH, TM_MIX), lambda i: (0, i)), row(RET_WIDTH + POOL_WIDTH), row(D_MODEL),
                  resident(w_out), _layer_spec(g_post, layer), _layer_spec(g_pre, layer),
                  resident(w_up), _layer_spec(cw, layer), _layer_spec(cb, layer)],
        out_specs=[row(D_MODEL), row(D_FF)],
        out_shape=[jax.ShapeDtypeStruct((n, D_MODEL), F32), jax.ShapeDtypeStruct((n, D_FF), BF16)],
        scratch_shapes=[
            pltpu.VMEM((SUBLANES, 2 * D_FF), F32),
            pltpu.VMEM((2, TM_MIX + SUBLANES, MXU_COLS), F32),
        ],
        compiler_params=_params("arbitrary"),
        name="mix_up",
    )(o_da, o_rp, x, w_out, g_post, g_pre, w_up, cw, cb)


def _down_proj_kernel(a_ref, x_ref, w_ref, g_ref, xo_ref, wb_ref):
    @pl.when(pl.program_id(0) == 0)
    def _():
        wb_ref[...] = w_ref[...].astype(BF16)

    y = jnp.dot(a_ref[...], wb_ref[...], preferred_element_type=F32)
    xo_ref[...] = x_ref[...] + _rms(y, g_ref[...])


def _down_proj(a, x, w, g, layer):
    n = x.shape[0]
    row = lambda width: pl.BlockSpec((TM_PROJ, width), lambda i: (i, 0))
    return pl.pallas_call(
        _down_proj_kernel,
        grid=(n // TM_PROJ,),
        in_specs=[row(D_FF), row(D_MODEL), _resident_spec(w, layer), _layer_spec(g, layer)],
        out_specs=row(D_MODEL),
        out_shape=jax.ShapeDtypeStruct((n, D_MODEL), F32),
        scratch_shapes=[pltpu.VMEM((D_FF, D_MODEL), BF16)],
        compiler_params=_params("arbitrary"),
        name="down_proj",
    )(a, x, w, g)


def _rotary_tables(seq, rot_dim, theta, period, tile):
    half = rot_dim // 2
    d = np.arange(LANES) % period
    rotated = d < rot_dim
    inv = jnp.float32(theta) ** (-jnp.arange(0, rot_dim, 2, dtype=F32) / rot_dim)
    inv_lane = jnp.where(jnp.asarray(rotated), inv[np.where(rotated, d % half, 0)], 0.0)
    start = (jnp.arange(seq // tile, dtype=F32) * tile)[:, None] * inv_lane[None, :]
    row = jnp.arange(tile, dtype=F32)[:, None] * inv_lane[None, :]
    masks = np.stack([(rotated & (d >= half)).astype(np.float32), -(rotated & (d < half)).astype(np.float32)])
    return (jnp.stack([jnp.cos(start), jnp.sin(start)]), jnp.stack([jnp.cos(row), jnp.sin(row)]),
            jnp.asarray(masks))


def _retention_tables():
    H, C = RET_HEADS, RET_CHUNK
    f32 = np.float32
    log_g = np.log(f32(1.0) - f32(2.0) ** (f32(-5.0) - np.arange(H, dtype=f32)))
    idx = np.arange(C, dtype=f32)
    diff = idx[:, None] - idx[None, :]
    decay = np.where(diff >= 0, np.exp(np.maximum(diff, f32(0.0)) * log_g[:, None, None]), f32(0.0))
    xi = np.exp((idx + f32(1.0)) * log_g[:, None])
    zeta = np.exp((f32(C) - f32(1.0) - idx) * log_g[:, None])
    chunk_decay = np.exp(f32(C) * log_g)
    xi_t = np.repeat(xi.T, RET_QK_DIM, axis=1)
    zeta_t = np.repeat(zeta.T, RET_QK_DIM, axis=1)
    row_head = np.arange(LANES) // RET_QK_DIM
    col_head = np.arange(RET_WIDTH) // RET_V_DIM
    bd = (row_head[:, None] == col_head[None, :]).astype(f32)
    cd = bd * chunk_decay[row_head][:, None]
    avg = (col_head[:, None] == col_head[None, :]).astype(f32) / f32(RET_V_DIM)
    win = np.repeat(np.asarray(POOL_WINDOWS, f32), POOL_DIM)[None, :]
    tabs = [jnp.asarray(t.astype(f32)) for t in (decay, xi_t, zeta_t, cd, bd)]
    return (*tabs, jnp.asarray(avg, BF16), jnp.asarray(win))


def _block_diag(w):
    L, G, P, _ = w.shape
    eye = jnp.eye(G, dtype=w.dtype)
    return (eye[None, :, None, :, None] * w[:, :, :, None, :]).reshape(L, G * P, G * P)


def kernel(x, norm_mix_pre, norm_mix_post, w_in, lambda_q1, lambda_k1, lambda_q2, lambda_k2, diff_subln,
           ret_norm, pool_w, pool_scale, w_out, norm_mlp_pre, norm_mlp_post, w_up, conv_w, conv_b, w_down):
    batch, seq, _ = x.shape
    depth = w_in.shape[0]
    assert seq % TQ == 0 and seq % TM_PROJ == 0 and seq % TM_MIX == 0
    xf = x.reshape(batch * seq, D_MODEL)

    tda = _rotary_tables(seq, DA_ROT_DIM, ROPE_THETA, DA_QK_DIM, TM_PROJ)
    tret = _rotary_tables(seq, RET_QK_DIM, RET_THETA, RET_QK_DIM, TM_PROJ)
    rtabs = _retention_tables()

    rows = lambda a: a.reshape(depth, 1, -1)
    w_out_b, w_up_b = w_out.astype(BF16), w_up.astype(BF16)
    pool_w_b = _block_diag(pool_w).astype(BF16)
    half_val = jnp.concatenate([jnp.ones((D_FF,), F32), jnp.full((D_FF,), 0.5, F32)])
    conv_w_s, conv_b_s = conv_w * half_val, rows(conv_b * half_val)
    g_mix_pre, g_mix_post, g_mlp_pre, g_mlp_post = (rows(g) for g in (norm_mix_pre, norm_mix_post,
                                                                      norm_mlp_pre, norm_mlp_post))
    lq1, lk1, lq2, lk2 = (rows(v) for v in (lambda_q1, lambda_k1, lambda_q2, lambda_k2))
    g_sub = diff_subln.reshape(depth, DA_V_DIM, 1)
    g_ret, p_scale = rows(ret_norm), rows(pool_scale)

    for l in range(depth):
        lam_init = 0.8 - 0.6 * math.exp(-0.3 * l)
        proj, o_rp = _in_mix(xf, g_mix_pre, w_in, tda, tret, rtabs, g_ret, pool_w_b, p_scale, seq, l)
        o_da = _diff_attn(proj, lq1, lk1, lq2, lk2, g_sub, lam_init, batch, seq, l)
        xf, a = _mix_up(o_da, o_rp, xf, w_out_b, g_mix_post, g_mlp_pre, w_up_b, conv_w_s, conv_b_s, seq, l)
        xf = _down_proj(a, xf, w_down, g_mlp_post, l)
    return xf.reshape(batch, seq, D_MODEL)
```

```python
import functools
import math

import jax
import jax.numpy as jnp
import numpy as np
from jax import lax
from jax.experimental import pallas as pl
from jax.experimental.pallas import tpu as pltpu

F32 = jnp.float32
BF16 = jnp.bfloat16

D_MODEL = 1024
DA_HEADS = 4
DA_QK_DIM = 64
DA_V_DIM = 128
DA_ROT_DIM = 16
ROPE_THETA = 500000.0
RET_HEADS = 4
RET_QK_DIM = 32
RET_V_DIM = 64
RET_THETA = 10000.0
RET_CHUNK = 128
POOL_GROUPS = 4
POOL_DIM = 64
POOL_WINDOWS = (2, 4, 8, 16)
POOL_HALO = 16
DA_WIDTH = 512
RET_WIDTH = 256
POOL_WIDTH = 256
IN_WIDTH = 2560
D_FF = 2816
CONV_WIDTH = 3
EPS = 1e-6

COL_Q_DA, COL_K_DA, COL_V_DA = 0, 512, 1024
COL_QK_R, COL_V_R, COL_G_R, COL_U = 1536, 1792, 2048, 2304

LANES = 128
SUBLANES = 8
MXU_COLS = 256
VMEM_LIMIT = 56 * 1024 * 1024

TM_PROJ = 1024
TM_MIX = 512
ROT_BLOCK = 128
TQ = 1024
TK = 512
SUB = 256
QK_AHEAD = 2
VT_ROWS = DA_V_DIM + 16
LOG2E = math.log2(math.e)
GELU_C = math.sqrt(2.0 / math.pi)


def _params(*sem):
    return pltpu.CompilerParams(dimension_semantics=sem, vmem_limit_bytes=VMEM_LIMIT)


def _layer_spec(arr, layer):
    tail = arr.shape[1:]
    return pl.BlockSpec((None,) + tail, lambda *_: (layer,) + (0,) * len(tail))


def _resident_spec(arr, layer):
    tail = arr.shape[1:]
    return pl.BlockSpec((None,) + tail, lambda *_: (layer,) + (0,) * len(tail), pipeline_mode=pl.Buffered(1))


def _rms(x, g):
    return x * lax.rsqrt(jnp.mean(x * x, axis=-1, keepdims=True) + EPS) * g


def _rot_tables(tabs, tile, rows):
    ta_ref, tb_ref, msk_ref = tabs
    cb, sb = tb_ref[0], tb_ref[1]
    blocks = rows // ROT_BLOCK
    cos, sin = [], []
    for k in range(blocks):
        ca = ta_ref[0, pl.ds(tile * blocks + k, 1), :]
        sa = ta_ref[1, pl.ds(tile * blocks + k, 1), :]
        cos.append(ca * cb - sa * sb)
        sin.append(sa * cb + ca * sb)
    c, s = jnp.concatenate(cos, axis=0), jnp.concatenate(sin, axis=0)
    return c, s * msk_ref[0:1, :], s * msk_ref[1:2, :]


def _rot(x, tabs, shift):
    return x * tabs[0] + pltpu.roll(x, shift, 1) * tabs[1] + pltpu.roll(x, LANES - shift, 1) * tabs[2]


def _in_mix_kernel(x_ref, g_ref, w_ref, *rest, tiles_per_seq):
    rot_refs, ret_consts, (o_ref, orp_ref), ret_scratch, wb_ref = (rest[:6], rest[6:16], rest[16:18],
                                                                   rest[18:21], rest[21])
    i = pl.program_id(0)

    @pl.when(i == 0)
    def _():
        wb_ref[...] = w_ref[...].astype(BF16)

    h = _rms(x_ref[...], g_ref[...]).astype(BF16)
    tda = _rot_tables(rot_refs[:3], i % tiles_per_seq, x_ref.shape[0])
    tret = _rot_tables(rot_refs[3:], i % tiles_per_seq, x_ref.shape[0])
    kept = {}

    def column(c0):
        p = jnp.dot(h, wb_ref[:, c0:c0 + MXU_COLS], preferred_element_type=F32)
        lo, hi = p[:, :LANES], p[:, LANES:]
        if c0 < COL_V_DA:
            lo = _rot(lo, tda, DA_ROT_DIM // 2)
            hi = _rot(hi, tda, DA_ROT_DIM // 2)
            if c0 < COL_K_DA:
                lo = lo * (DA_QK_DIM ** -0.5 * LOG2E)
                hi = hi * (DA_QK_DIM ** -0.5 * LOG2E)
        elif c0 == COL_QK_R:
            lo = _rot(lo, tret, RET_QK_DIM // 2)
            hi = _rot(hi, tret, RET_QK_DIM // 2) * (RET_QK_DIM ** -0.5)
        if c0 < COL_QK_R:
            o_ref[:, c0:c0 + LANES] = lo.astype(BF16)
            o_ref[:, c0 + LANES:c0 + MXU_COLS] = hi.astype(BF16)
        else:
            kept[c0] = jnp.concatenate([lo.astype(BF16), hi.astype(BF16)], axis=1)

    for c0 in range(COL_QK_R, IN_WIDTH, MXU_COLS):
        column(c0)
    fillers = [functools.partial(column, c0) for c0 in range(0, COL_QK_R, MXU_COLS)]
    _ret_pool_body(i % tiles_per_seq == 0, (i % tiles_per_seq) * x_ref.shape[0],
                   kept[COL_QK_R], kept[COL_V_R], kept[COL_G_R], kept[COL_U],
                   *ret_consts, orp_ref, *ret_scratch, fillers=fillers)
    for f in fillers:
        f()


def _in_mix(x, g, w, tda, tret, rtabs, ret_g, pool_w_bd, pool_scale, seq, layer):
    n = x.shape[0]
    tiles_per_seq = seq // TM_PROJ
    decay, xi, zeta, cd, bd, avg, win = rtabs
    const = lambda a: pl.BlockSpec(a.shape, lambda i: (0,) * a.ndim)
    row = lambda width: pl.BlockSpec((TM_PROJ, width), lambda i: (i, 0))
    return pl.pallas_call(
        functools.partial(_in_mix_kernel, tiles_per_seq=tiles_per_seq),
        grid=(n // TM_PROJ,),
        in_specs=[
            row(D_MODEL),
            _layer_spec(g, layer),
            _resident_spec(w, layer),
            *(const(t) for t in tda), *(const(t) for t in tret),
            const(decay), const(xi), const(zeta), const(cd), const(bd), const(avg),
            _layer_spec(ret_g, layer), const(win), _layer_spec(pool_w_bd, layer), _layer_spec(pool_scale, layer),
        ],
        out_specs=[row(COL_QK_R), row(RET_WIDTH + POOL_WIDTH)],
        out_shape=[jax.ShapeDtypeStruct((n, COL_QK_R), BF16),
                   jax.ShapeDtypeStruct((n, RET_WIDTH + POOL_WIDTH), BF16)],
        scratch_shapes=[
            pltpu.VMEM((LANES, RET_WIDTH), F32),
            pltpu.VMEM((POOL_HALO, POOL_WIDTH), F32),
            pltpu.VMEM((TM_PROJ, RET_WIDTH), F32),
            pltpu.VMEM((D_MODEL, IN_WIDTH), BF16),
        ],
        compiler_params=_params("arbitrary"),
        name="in_mix",
    )(x, g, w, *tda, *tret, decay, xi, zeta, cd, bd, avg, ret_g, win, pool_w_bd, pool_scale)


def _diff_attn_kernel(q_ref, qn_ref, k_ref, v_ref, lq1_ref, lk1_ref, lq2_ref, lk2_ref, g_ref, o_ref,
                      vt_ref, qt_ref, qtn_ref, s0_ref, s1_ref, mx0_ref, mx1_ref, m_ref, acc_ref, *, lam_init):
    qi = pl.program_id(2)
    nkv = v_ref.shape[0] // TK
    s_refs, mx_refs = (s0_ref, s1_ref), (mx0_ref, mx1_ref)

    @pl.when(qi == 0)
    def _():
        for c in range(nkv):
            vt_ref[c, :DA_V_DIM, :] = v_ref[c * TK:(c + 1) * TK, :].astype(F32).T.astype(BF16)
            vt_ref[c, DA_V_DIM:, :] = jnp.ones((VT_ROWS - DA_V_DIM, TK), BF16)

    def store_qt(dst_ref, src_ref):
        qt = src_ref[...].astype(F32).T
        feat = lax.broadcasted_iota(jnp.int32, qt.shape, 0)
        dst_ref[0] = jnp.where(feat < DA_QK_DIM, qt, 0.0).astype(BF16)
        dst_ref[1] = jnp.where(feat >= DA_QK_DIM, qt, 0.0).astype(BF16)

    m_ref[...] = jnp.full(m_ref.shape, -jnp.inf, F32)
    acc_ref[...] = jnp.zeros(acc_ref.shape, F32)

    def columns(diag):
        out = []
        for qb in range(TQ // SUB):
            if diag is not None and diag * TK > (qb + 1) * SUB - 1:
                continue
            mask = diag is not None and (diag + 1) * TK - 1 > qb * SUB
            out += [(2 * qb + mi, qb, mi, mask) for mi in range(2)]
        return out

    def score(c, slot, diag, q_src=qt_ref):
        def one(j, qb, mi, mask):
            ks = k_ref[pl.ds(pl.multiple_of(c * TK, TK), TK), :]
            st = jnp.dot(ks, q_src[mi, :, qb * SUB:(qb + 1) * SUB],
                         preferred_element_type=F32)
            if mask:
                key = diag * TK + lax.broadcasted_iota(jnp.int32, st.shape, 0)
                qry = qb * SUB + lax.broadcasted_iota(jnp.int32, st.shape, 1)
                st = jnp.where(key <= qry, st, -jnp.inf)
            s_refs[slot][j] = st
            mx_refs[slot][j] = jnp.max(st, axis=0, keepdims=True)
        return [functools.partial(one, *col) for col in columns(diag)]

    def update(c, slot, diag):
        def one(j, qb, mi, mask):
            qs = slice(qb * SUB, (qb + 1) * SUB)
            m_prev = m_ref[mi, :, qs]
            m_new = jnp.maximum(m_prev, mx_refs[slot][j])
            alpha = jnp.exp2(m_prev - m_new)
            p = jnp.exp2(s_refs[slot][j] - m_new).astype(BF16)
            acc_ref[mi, :, qs] = (acc_ref[mi, :, qs] * alpha
                                  + jnp.dot(vt_ref[c], p, preferred_element_type=F32))
            m_ref[mi, :, qs] = m_new
        return [functools.partial(one, *col) for col in columns(diag)]

    def emit(scores, updates):
        scores = list(scores)
        for s in scores[:QK_AHEAD]:
            s()
        rest = scores[QK_AHEAD:]
        for u in updates:
            u()
            if rest:
                rest.pop(0)()
        for s in rest:
            s()

    assert TQ // TK == 2
    c0 = 2 * qi

    @pl.when(qi > 0)
    def _():
        qt_ref[...] = qtn_ref[...]

        def body(i, carry):
            emit(score(2 * i + 1, 1, None), update(2 * i, 0, None))
            emit(score(2 * i + 2, 0, None), update(2 * i + 1, 1, None))
            return carry

        lax.fori_loop(0, qi - 1, body, 0)
        emit(score(c0 - 1, 1, None), update(c0 - 2, 0, None))
        emit(score(c0, 0, 0), update(c0 - 1, 1, None))

    @pl.when(qi == 0)
    def _():
        store_qt(qt_ref, q_ref)
        emit(score(c0, 0, 0), [])

    store_qt(qtn_ref, qn_ref)
    emit(score(c0 + 1, 1, 1), update(c0, 0, 0))
    last_tile = qi == pl.num_programs(2) - 1

    @pl.when(jnp.logical_not(last_tile))
    def _():
        emit(score(0, 0, None, qtn_ref), update(c0 + 1, 1, 1))

    @pl.when(last_tile)
    def _():
        emit([], update(c0 + 1, 1, 1))

    lam = (jnp.exp(jnp.sum(lq1_ref[...] * lk1_ref[...], axis=-1, keepdims=True))
           - jnp.exp(jnp.sum(lq2_ref[...] * lk2_ref[...], axis=-1, keepdims=True)) + lam_init)
    o1 = acc_ref[0, :DA_V_DIM, :] / acc_ref[0, DA_V_DIM:DA_V_DIM + 1, :]
    o2 = acc_ref[1, :DA_V_DIM, :] / acc_ref[1, DA_V_DIM:DA_V_DIM + 1, :]
    a = o1 - lam * o2
    y = a * lax.rsqrt(jnp.mean(a * a, axis=0, keepdims=True) + EPS) * g_ref[...]
    o_ref[...] = (y * (1.0 - lam_init)).astype(BF16)


def _diff_attn(proj, lq1, lk1, lq2, lk2, g, lam_init, batch, seq, layer):
    n = proj.shape[0]
    nq = seq // TQ
    kcol, vcol = COL_K_DA // LANES, COL_V_DA // LANES
    return pl.pallas_call(
        functools.partial(_diff_attn_kernel, lam_init=lam_init),
        grid=(batch, DA_HEADS, nq),
        in_specs=[
            pl.BlockSpec((TQ, LANES), lambda b, h, i: (b * nq + i, h)),
            pl.BlockSpec((TQ, LANES), lambda b, h, i: (b * nq + jnp.minimum(i + 1, nq - 1), h)),
            pl.BlockSpec((seq, LANES), lambda b, h, i: (b, kcol + h)),
            pl.BlockSpec((seq, LANES), lambda b, h, i: (b, vcol + h)),
            _layer_spec(lq1, layer), _layer_spec(lk1, layer), _layer_spec(lq2, layer), _layer_spec(lk2, layer),
            _layer_spec(g, layer),
        ],
        out_specs=pl.BlockSpec((LANES, TQ), lambda b, h, i: (h, b * nq + i)),
        out_shape=jax.ShapeDtypeStruct((DA_WIDTH, n), BF16),
        scratch_shapes=[
            pltpu.VMEM((seq // TK, VT_ROWS, TK), BF16),
            pltpu.VMEM((2, LANES, TQ), BF16),
            pltpu.VMEM((2, LANES, TQ), BF16),
            pltpu.VMEM((2 * TQ // SUB, TK, SUB), F32),
            pltpu.VMEM((2 * TQ // SUB, TK, SUB), F32),
            pltpu.VMEM((2 * TQ // SUB, 1, SUB), F32),
            pltpu.VMEM((2 * TQ // SUB, 1, SUB), F32),
            pltpu.VMEM((2, 1, TQ), F32),
            pltpu.VMEM((2, VT_ROWS, TQ), F32),
        ],
        compiler_params=_params("arbitrary", "arbitrary", "arbitrary"),
        name="diff_attn",
    )(proj, proj, proj, proj, lq1, lk1, lq2, lk2, g)


def _split_dot(x, w):
    hi = x.astype(BF16)
    lo = (x - hi.astype(F32)).astype(BF16)
    return (jnp.dot(hi, w, preferred_element_type=F32) + jnp.dot(lo, w, preferred_element_type=F32))


def _ret_pool_body(seq_start, t0, qk, v, g, u, decay_ref, xi_ref, zeta_ref, cd_ref, bd_ref,
                   avg_ref, retg_ref, win_ref, pw_ref, ps_ref, o_ref, state_ref, halo_ref, oret_ref, *, fillers):
    rows = qk.shape[0]

    def fill():
        if fillers:
            fillers.pop(0)()

    @pl.when(seq_start)
    def _():
        state_ref[...] = jnp.zeros(state_ref.shape, F32)
        halo_ref[...] = jnp.zeros(halo_ref.shape, F32)

    C = RET_CHUNK
    chunks = range(rows // C)
    qlane = lax.broadcasted_iota(jnp.int32, (C, LANES), 1) // RET_QK_DIM
    vlane = lax.broadcasted_iota(jnp.int32, (C, RET_WIDTH), 1) // RET_V_DIM
    qs = [qk[c * C:(c + 1) * C, :LANES] for c in chunks]
    ks = [qk[c * C:(c + 1) * C, LANES:] for c in chunks]
    vs = [v[c * C:(c + 1) * C, :] for c in chunks]
    inner = [[lax.dot_general(jnp.where(qlane == h, qs[c], jnp.zeros_like(qs[c])), ks[c],
                              (((1,), (1,)), ((), ())), preferred_element_type=F32)
              for h in range(RET_HEADS)] for c in chunks]
    upd = [lax.dot_general((ks[c].astype(F32) * zeta_ref[...]).astype(BF16), vs[c],
                           (((0,), (0,)), ((), ())), preferred_element_type=F32) for c in chunks]
    fill()
    states = [state_ref[...]]
    for c in chunks:
        states.append(states[c] * cd_ref[...] + bd_ref[...] * upd[c])
    state_ref[...] = states[-1]
    for c in chunks:
        o_c = jnp.dot((qs[c].astype(F32) * xi_ref[...]).astype(BF16), states[c].astype(BF16),
                      preferred_element_type=F32)
        for h in range(RET_HEADS):
            vh = jnp.where(vlane == h, vs[c], jnp.zeros_like(vs[c]))
            o_c = o_c + jnp.dot((inner[c][h] * decay_ref[h]).astype(BF16), vh, preferred_element_type=F32)
        oret_ref[c * C:(c + 1) * C, :] = o_c
        if c % 2 == 1:
            fill()

    o = oret_ref[...]
    mu = _split_dot(o, avg_ref[...])
    d = o - mu
    var = jnp.dot((d * d).astype(BF16), avg_ref[...], preferred_element_type=F32)
    y = d * lax.rsqrt(var + EPS) * retg_ref[...]
    gate = g.astype(F32)
    o_ref[:, :RET_WIDTH] = (gate * jax.nn.sigmoid(gate) * y).astype(BF16)
    fill()

    u = u.astype(F32)
    halo_groups = POOL_HALO // SUBLANES
    ext = jnp.concatenate([halo_ref[...], u], axis=0).reshape(-1, SUBLANES, POOL_WIDTH)
    halo_ref[...] = u[rows - POOL_HALO:, :]

    def shift_down(x, k):
        padded = jnp.concatenate([jnp.zeros_like(x[:1]), x], axis=0)
        if k == SUBLANES:
            return padded[:-1]
        r = pltpu.roll(padded, k, 1)
        return jnp.where(lax.broadcasted_iota(jnp.int32, x.shape, 1) < k, r[:-1], r[1:])

    assert POOL_WINDOWS == (2, 4, 8, 16) and POOL_DIM * 2 == LANES
    s2 = ext + shift_down(ext, 1)
    s4 = s2 + shift_down(s2, 2)
    s4_hi = s4[:, :, LANES:]
    s8 = s4_hi + shift_down(s4_hi, 4)
    s16 = s8 + shift_down(s8, 8)
    first = lax.broadcasted_iota(jnp.int32, s8.shape, 2) < POOL_DIM
    psum = jnp.concatenate([jnp.where(first, s2[:, :, :LANES], s4[:, :, :LANES]), jnp.where(first, s8, s16)],
                           axis=2)[halo_groups:].reshape(rows, POOL_WIDTH)
    t = (t0 + lax.broadcasted_iota(jnp.int32, u.shape, 0)).astype(F32)
    pooled = psum / jnp.minimum(t + 1.0, win_ref[...]) - u
    yp = jnp.dot(pooled.astype(BF16), pw_ref[...], preferred_element_type=F32) * ps_ref[...]
    o_ref[:, RET_WIDTH:] = yp.astype(BF16)


def _mix_up_kernel(oda_ref, orp_ref, x_ref, wo_ref, gpost_ref, gpre_ref, w_ref, cw_ref, cb_ref,
                   xo_ref, a_ref, carry_ref, ext_ref, *, tiles_per_seq):
    i = pl.program_id(0)
    tm = x_ref.shape[0]

    @pl.when(i % tiles_per_seq == 0)
    def _():
        carry_ref[...] = jnp.zeros(carry_ref.shape, F32)

    mix = (lax.dot_general(oda_ref[...], wo_ref[:DA_WIDTH, :], (((0,), (0,)), ((), ())),
                           preferred_element_type=F32)
           + jnp.dot(orp_ref[...], wo_ref[DA_WIDTH:, :], preferred_element_type=F32))
    xn = x_ref[...] + _rms(mix, gpost_ref[...])
    xo_ref[...] = xn
    h = _rms(xn, gpre_ref[...]).astype(BF16)

    def conv_cols(slot, c0):
        u = jnp.dot(h, w_ref[:, c0:c0 + MXU_COLS], preferred_element_type=F32)
        ext_ref[slot, :SUBLANES, :] = carry_ref[:, c0:c0 + MXU_COLS]
        ext_ref[slot, SUBLANES:, :] = u
        carry_ref[:, c0:c0 + MXU_COLS] = u[tm - SUBLANES:, :]
        cw = cw_ref[:, c0:c0 + MXU_COLS]
        return (cb_ref[:, c0:c0 + MXU_COLS]
                + cw[0:1] * ext_ref[slot, SUBLANES - 2:SUBLANES - 2 + tm, :]
                + cw[1:2] * ext_ref[slot, SUBLANES - 1:SUBLANES - 1 + tm, :]
                + cw[2:3] * u)

    for c in range(D_FF // MXU_COLS):
        c0 = c * MXU_COLS
        g = conv_cols(0, c0)
        v = conv_cols(1, D_FF + c0)
        th = jnp.tanh(g * (g * g * (GELU_C * 0.044715) + GELU_C))
        a_ref[:, c0:c0 + MXU_COLS] = (g * v * (th + 1.0)).astype(BF16)


def _mix_up(o_da, o_rp, x, w_out, g_post, g_pre, w_up, cw, cb, seq, layer):
    n = x.shape[0]
    row = lambda width: pl.BlockSpec((TM_MIX, width), lambda i: (i, 0))
    resident = functools.partial(_resident_spec, layer=layer)
    return pl.pallas_call(
        functools.partial(_mix_up_kernel, tiles_per_seq=seq // TM_MIX),
        grid=(n // TM_MIX,),
        in_specs=[pl.BlockSpec((DA_WIDTH, TM_MIX), lambda i: (0, i)), row(RET_WIDTH + POOL_WIDTH), row(D_MODEL),
                  resident(w_out), _layer_spec(g_post, layer), _layer_spec(g_pre, layer),
                  resident(w_up), _layer_spec(cw, layer), _layer_spec(cb, layer)],
        out_specs=[row(D_MODEL), row(D_FF)],
        out_shape=[jax.ShapeDtypeStruct((n, D_MODEL), F32), jax.ShapeDtypeStruct((n, D_FF), BF16)],
        scratch_shapes=[
            pltpu.VMEM((SUBLANES, 2 * D_FF), F32),
            pltpu.VMEM((2, TM_MIX + SUBLANES, MXU_COLS), F32),
        ],
        compiler_params=_params("arbitrary"),
        name="mix_up",
    )(o_da, o_rp, x, w_out, g_post, g_pre, w_up, cw, cb)


def _down_proj_kernel(a_ref, x_ref, w_ref, g_ref, xo_ref, wb_ref):
    @pl.when(pl.program_id(0) == 0)
    def _():
        wb_ref[...] = w_ref[...].astype(BF16)

    y = jnp.dot(a_ref[...], wb_ref[...], preferred_element_type=F32)
    xo_ref[...] = x_ref[...] + _rms(y, g_ref[...])


def _down_proj(a, x, w, g, layer):
    n = x.shape[0]
    row = lambda width: pl.BlockSpec((TM_PROJ, width), lambda i: (i, 0))
    return pl.pallas_call(
        _down_proj_kernel,
        grid=(n // TM_PROJ,),
        in_specs=[row(D_FF), row(D_MODEL), _resident_spec(w, layer), _layer_spec(g, layer)],
        out_specs=row(D_MODEL),
        out_shape=jax.ShapeDtypeStruct((n, D_MODEL), F32),
        scratch_shapes=[pltpu.VMEM((D_FF, D_MODEL), BF16)],
        compiler_params=_params("arbitrary"),
        name="down_proj",
    )(a, x, w, g)


def _rotary_tables(seq, rot_dim, theta, period, tile):
    half = rot_dim // 2
    d = np.arange(LANES) % period
    rotated = d < rot_dim
    inv = jnp.float32(theta) ** (-jnp.arange(0, rot_dim, 2, dtype=F32) / rot_dim)
    inv_lane = jnp.where(jnp.asarray(rotated), inv[np.where(rotated, d % half, 0)], 0.0)
    start = (jnp.arange(seq // tile, dtype=F32) * tile)[:, None] * inv_lane[None, :]
    row = jnp.arange(tile, dtype=F32)[:, None] * inv_lane[None, :]
    masks = np.stack([(rotated & (d >= half)).astype(np.float32), -(rotated & (d < half)).astype(np.float32)])
    return (jnp.stack([jnp.cos(start), jnp.sin(start)]), jnp.stack([jnp.cos(row), jnp.sin(row)]),
            jnp.asarray(masks))


def _retention_tables():
    H, C = RET_HEADS, RET_CHUNK
    f32 = np.float32
    log_g = np.log(f32(1.0) - f32(2.0) ** (f32(-5.0) - np.arange(H, dtype=f32)))
    idx = np.arange(C, dtype=f32)
    diff = idx[:, None] - idx[None, :]
    decay = np.where(diff >= 0, np.exp(np.maximum(diff, f32(0.0)) * log_g[:, None, None]), f32(0.0))
    xi = np.exp((idx + f32(1.0)) * log_g[:, None])
    zeta = np.exp((f32(C) - f32(1.0) - idx) * log_g[:, None])
    chunk_decay = np.exp(f32(C) * log_g)
    xi_t = np.repeat(xi.T, RET_QK_DIM, axis=1)
    zeta_t = np.repeat(zeta.T, RET_QK_DIM, axis=1)
    row_head = np.arange(LANES) // RET_QK_DIM
    col_head = np.arange(RET_WIDTH) // RET_V_DIM
    bd = (row_head[:, None] == col_head[None, :]).astype(f32)
    cd = bd * chunk_decay[row_head][:, None]
    avg = (col_head[:, None] == col_head[None, :]).astype(f32) / f32(RET_V_DIM)
    win = np.repeat(np.asarray(POOL_WINDOWS, f32), POOL_DIM)[None, :]
    tabs = [jnp.asarray(t.astype(f32)) for t in (decay, xi_t, zeta_t, cd, bd)]
    return (*tabs, jnp.asarray(avg, BF16), jnp.asarray(win))


def _block_diag(w):
    L, G, P, _ = w.shape
    eye = jnp.eye(G, dtype=w.dtype)
    return (eye[None, :, None, :, None] * w[:, :, :, None, :]).reshape(L, G * P, G * P)


def kernel(x, norm_mix_pre, norm_mix_post, w_in, lambda_q1, lambda_k1, lambda_q2, lambda_k2, diff_subln,
           ret_norm, pool_w, pool_scale, w_out, norm_mlp_pre, norm_mlp_post, w_up, conv_w, conv_b, w_down):
    batch, seq, _ = x.shape
    depth = w_in.shape[0]
    assert seq % TQ == 0 and seq % TM_PROJ == 0 and seq % TM_MIX == 0
    xf = x.reshape(batch * seq, D_MODEL)

    tda = _rotary_tables(seq, DA_ROT_DIM, ROPE_THETA, DA_QK_DIM, ROT_BLOCK)
    tret = _rotary_tables(seq, RET_QK_DIM, RET_THETA, RET_QK_DIM, ROT_BLOCK)
    rtabs = _retention_tables()

    rows = lambda a: a.reshape(depth, 1, -1)
    w_out_b, w_up_b = w_out.astype(BF16), w_up.astype(BF16)
    pool_w_b = _block_diag(pool_w).astype(BF16)
    half_val = jnp.concatenate([jnp.ones((D_FF,), F32), jnp.full((D_FF,), 0.5, F32)])
    conv_w_s, conv_b_s = conv_w * half_val, rows(conv_b * half_val)
    g_mix_pre, g_mix_post, g_mlp_pre, g_mlp_post = (rows(g) for g in (norm_mix_pre, norm_mix_post,
                                                                      norm_mlp_pre, norm_mlp_post))
    lq1, lk1, lq2, lk2 = (rows(v) for v in (lambda_q1, lambda_k1, lambda_q2, lambda_k2))
    g_sub = diff_subln.reshape(depth, DA_V_DIM, 1)
    g_ret, p_scale = rows(ret_norm), rows(pool_scale)

    for l in range(depth):
        lam_init = 0.8 - 0.6 * math.exp(-0.3 * l)
        proj, o_rp = _in_mix(xf, g_mix_pre, w_in, tda, tret, rtabs, g_ret, pool_w_b, p_scale, seq, l)
        o_da = _diff_attn(proj, lq1, lk1, lq2, lk2, g_sub, lam_init, batch, seq, l)
        xf, a = _mix_up(o_da, o_rp, xf, w_out_b, g_mix_post, g_mlp_pre, w_up_b, conv_w_s, conv_b_s, seq, l)
        xf = _down_proj(a, xf, w_down, g_mlp_post, l)
    return xf.reshape(batch, seq, D_MODEL)
```

```python
import functools
import math

import jax
import jax.numpy as jnp
import numpy as np
from jax import lax
from jax.experimental import pallas as pl
from jax.experimental.pallas import tpu as pltpu

F32 = jnp.float32
BF16 = jnp.bfloat16

D_MODEL = 1024
DA_HEADS = 4
DA_QK_DIM = 64
DA_V_DIM = 128
DA_ROT_DIM = 16
ROPE_THETA = 500000.0
RET_HEADS = 4
RET_QK_DIM = 32
RET_V_DIM = 64
RET_THETA = 10000.0
RET_CHUNK = 128
POOL_GROUPS = 4
POOL_DIM = 64
POOL_WINDOWS = (2, 4, 8, 16)
POOL_HALO = 16
DA_WIDTH = 512
RET_WIDTH = 256
POOL_WIDTH = 256
IN_WIDTH = 2560
D_FF = 2816
CONV_WIDTH = 3
EPS = 1e-6

COL_Q_DA, COL_K_DA, COL_V_DA = 0, 512, 1024
COL_QK_R, COL_V_R, COL_G_R, COL_U = 1536, 1792, 2048, 2304

LANES = 128
SUBLANES = 8
MXU_COLS = 256
VMEM_LIMIT = 56 * 1024 * 1024

TM_PROJ = 1024
TM_MIX = 512
ROT_BLOCK = 128
TQ = 1024
TK = 512
SUB = 256
QK_AHEAD = 2
VT_ROWS = DA_V_DIM + 16
LOG2E = math.log2(math.e)
GELU_C = math.sqrt(2.0 / math.pi)


def _params(*sem):
    return pltpu.CompilerParams(dimension_semantics=sem, vmem_limit_bytes=VMEM_LIMIT)


def _layer_spec(arr, layer):
    tail = arr.shape[1:]
    return pl.BlockSpec((None,) + tail, lambda *_: (layer,) + (0,) * len(tail))


def _resident_spec(arr, layer):
    tail = arr.shape[1:]
    return pl.BlockSpec((None,) + tail, lambda *_: (layer,) + (0,) * len(tail), pipeline_mode=pl.Buffered(1))


def _rms(x, g):
    return x * lax.rsqrt(jnp.mean(x * x, axis=-1, keepdims=True) + EPS) * g


def _rot_tables(tabs, tile, rows):
    ta_ref, tb_ref, msk_ref = tabs
    cb, sb = tb_ref[0], tb_ref[1]
    blocks = rows // ROT_BLOCK
    cos, sin = [], []
    for k in range(blocks):
        ca = ta_ref[0, pl.ds(tile * blocks + k, 1), :]
        sa = ta_ref[1, pl.ds(tile * blocks + k, 1), :]
        cos.append(ca * cb - sa * sb)
        sin.append(sa * cb + ca * sb)
    c, s = jnp.concatenate(cos, axis=0), jnp.concatenate(sin, axis=0)
    return c, s * msk_ref[0:1, :], s * msk_ref[1:2, :]


def _rot(x, tabs, shift):
    return x * tabs[0] + pltpu.roll(x, shift, 1) * tabs[1] + pltpu.roll(x, LANES - shift, 1) * tabs[2]


def _in_mix_kernel(x_ref, g_ref, w_ref, *rest, tiles_per_seq):
    rot_refs, ret_consts, (o_ref, orp_ref), ret_scratch, wb_ref = (rest[:6], rest[6:16], rest[16:18],
                                                                   rest[18:21], rest[21])
    i = pl.program_id(0)

    @pl.when(i == 0)
    def _():
        wb_ref[...] = w_ref[...].astype(BF16)

    h = _rms(x_ref[...], g_ref[...]).astype(BF16)
    tda = _rot_tables(rot_refs[:3], i % tiles_per_seq, x_ref.shape[0])
    tret = _rot_tables(rot_refs[3:], i % tiles_per_seq, x_ref.shape[0])
    kept = {}

    def column(c0):
        p = jnp.dot(h, wb_ref[:, c0:c0 + MXU_COLS], preferred_element_type=F32)
        lo, hi = p[:, :LANES], p[:, LANES:]
        if c0 < COL_V_DA:
            lo = _rot(lo, tda, DA_ROT_DIM // 2)
            hi = _rot(hi, tda, DA_ROT_DIM // 2)
            if c0 < COL_K_DA:
                lo = lo * (DA_QK_DIM ** -0.5 * LOG2E)
                hi = hi * (DA_QK_DIM ** -0.5 * LOG2E)
        elif c0 == COL_QK_R:
            lo = _rot(lo, tret, RET_QK_DIM // 2)
            hi = _rot(hi, tret, RET_QK_DIM // 2) * (RET_QK_DIM ** -0.5)
        if c0 < COL_QK_R:
            o_ref[:, c0:c0 + LANES] = lo.astype(BF16)
            o_ref[:, c0 + LANES:c0 + MXU_COLS] = hi.astype(BF16)
        else:
            kept[c0] = jnp.concatenate([lo.astype(BF16), hi.astype(BF16)], axis=1)

    for c0 in range(COL_QK_R, IN_WIDTH, MXU_COLS):
        column(c0)
    fillers = [functools.partial(column, c0) for c0 in range(0, COL_QK_R, MXU_COLS)]
    _ret_pool_body(i % tiles_per_seq == 0, (i % tiles_per_seq) * x_ref.shape[0],
                   kept[COL_QK_R], kept[COL_V_R], kept[COL_G_R], kept[COL_U],
                   *ret_consts, orp_ref, *ret_scratch, fillers=fillers)
    for f in fillers:
        f()


def _in_mix(x, g, w, tda, tret, rtabs, ret_g, pool_w_bd, pool_scale, seq, layer):
    n = x.shape[0]
    tiles_per_seq = seq // TM_PROJ
    decay, xi, zeta, cd, bd, avg, win = rtabs
    const = lambda a: pl.BlockSpec(a.shape, lambda i: (0,) * a.ndim)
    row = lambda width: pl.BlockSpec((TM_PROJ, width), lambda i: (i, 0))
    return pl.pallas_call(
        functools.partial(_in_mix_kernel, tiles_per_seq=tiles_per_seq),
        grid=(n // TM_PROJ,),
        in_specs=[
            row(D_MODEL),
            _layer_spec(g, layer),
            _resident_spec(w, layer),
            *(const(t) for t in tda), *(const(t) for t in tret),
            const(decay), const(xi), const(zeta), const(cd), const(bd), const(avg),
            _layer_spec(ret_g, layer), const(win), _layer_spec(pool_w_bd, layer), _layer_spec(pool_scale, layer),
        ],
        out_specs=[row(COL_QK_R), row(RET_WIDTH + POOL_WIDTH)],
        out_shape=[jax.ShapeDtypeStruct((n, COL_QK_R), BF16),
                   jax.ShapeDtypeStruct((n, RET_WIDTH + POOL_WIDTH), BF16)],
        scratch_shapes=[
            pltpu.VMEM((LANES, RET_WIDTH), F32),
            pltpu.VMEM((POOL_HALO, POOL_WIDTH), F32),
            pltpu.VMEM((TM_PROJ, RET_WIDTH), F32),
            pltpu.VMEM((D_MODEL, IN_WIDTH), BF16),
        ],
        compiler_params=_params("arbitrary"),
        name="in_mix",
    )(x, g, w, *tda, *tret, decay, xi, zeta, cd, bd, avg, ret_g, win, pool_w_bd, pool_scale)


def _diff_attn_kernel(q_ref, qn_ref, k_ref, v_ref, lq1_ref, lk1_ref, lq2_ref, lk2_ref, g_ref, o_ref,
                      vt_ref, qt_ref, qtn_ref, s0_ref, s1_ref, mx0_ref, mx1_ref, m_ref, acc_ref, *, lam_init):
    qi = pl.program_id(2)
    nkv = v_ref.shape[0] // TK
    s_refs, mx_refs = (s0_ref, s1_ref), (mx0_ref, mx1_ref)

    @pl.when(qi == 0)
    def _():
        for c in range(nkv):
            vt_ref[c, :DA_V_DIM, :] = v_ref[c * TK:(c + 1) * TK, :].astype(F32).T.astype(BF16)
            vt_ref[c, DA_V_DIM:, :] = jnp.ones((VT_ROWS - DA_V_DIM, TK), BF16)

    def store_qt(dst_ref, src_ref):
        qt = src_ref[...].astype(F32).T
        feat = lax.broadcasted_iota(jnp.int32, qt.shape, 0)
        dst_ref[0] = jnp.where(feat < DA_QK_DIM, qt, 0.0).astype(BF16)
        dst_ref[1] = jnp.where(feat >= DA_QK_DIM, qt, 0.0).astype(BF16)

    m_ref[...] = jnp.full(m_ref.shape, -jnp.inf, F32)
    acc_ref[...] = jnp.zeros(acc_ref.shape, F32)

    def columns(diag):
        out = []
        for qb in range(TQ // SUB):
            if diag is not None and diag * TK > (qb + 1) * SUB - 1:
                continue
            mask = diag is not None and (diag + 1) * TK - 1 > qb * SUB
            out += [(2 * qb + mi, qb, mi, mask) for mi in range(2)]
        return out

    def score(c, slot, diag, q_src=qt_ref):
        def one(j, qb, mi, mask):
            ks = k_ref[pl.ds(pl.multiple_of(c * TK, TK), TK), :]
            st = jnp.dot(ks, q_src[mi, :, qb * SUB:(qb + 1) * SUB],
                         preferred_element_type=F32)
            if mask:
                key = diag * TK + lax.broadcasted_iota(jnp.int32, st.shape, 0)
                qry = qb * SUB + lax.broadcasted_iota(jnp.int32, st.shape, 1)
                st = jnp.where(key <= qry, st, -jnp.inf)
            s_refs[slot][j] = st
            mx_refs[slot][j] = jnp.max(st, axis=0, keepdims=True)
        return [functools.partial(one, *col) for col in columns(diag)]

    def update(c, slot, diag):
        def one(j, qb, mi, mask):
            qs = slice(qb * SUB, (qb + 1) * SUB)
            m_prev = m_ref[mi, :, qs]
            m_new = jnp.maximum(m_prev, mx_refs[slot][j])
            alpha = jnp.exp2(m_prev - m_new)
            p = jnp.exp2(s_refs[slot][j] - m_new).astype(BF16)
            acc_ref[mi, :, qs] = (acc_ref[mi, :, qs] * alpha
                                  + jnp.dot(vt_ref[c], p, preferred_element_type=F32))
            m_ref[mi, :, qs] = m_new
        return [functools.partial(one, *col) for col in columns(diag)]

    def emit(scores, updates):
        scores = list(scores)
        for s in scores[:QK_AHEAD]:
            s()
        rest = scores[QK_AHEAD:]
        for u in updates:
            u()
            if rest:
                rest.pop(0)()
        for s in rest:
            s()

    assert TQ // TK == 2
    c0 = 2 * qi

    @pl.when(qi > 0)
    def _():
        qt_ref[...] = qtn_ref[...]

        def body(i, carry):
            emit(score(2 * i + 1, 1, None), update(2 * i, 0, None))
            emit(score(2 * i + 2, 0, None), update(2 * i + 1, 1, None))
            return carry

        lax.fori_loop(0, qi - 1, body, 0)
        emit(score(c0 - 1, 1, None), update(c0 - 2, 0, None))
        emit(score(c0, 0, 0), update(c0 - 1, 1, None))

    @pl.when(qi == 0)
    def _():
        store_qt(qt_ref, q_ref)
        emit(score(c0, 0, 0), [])

    store_qt(qtn_ref, qn_ref)
    emit(score(c0 + 1, 1, 1), update(c0, 0, 0))
    last_tile = qi == pl.num_programs(2) - 1

    @pl.when(jnp.logical_not(last_tile))
    def _():
        emit(score(0, 0, None, qtn_ref), update(c0 + 1, 1, 1))

    @pl.when(last_tile)
    def _():
        emit([], update(c0 + 1, 1, 1))

    lam = (jnp.exp(jnp.sum(lq1_ref[...] * lk1_ref[...], axis=-1, keepdims=True))
           - jnp.exp(jnp.sum(lq2_ref[...] * lk2_ref[...], axis=-1, keepdims=True)) + lam_init)
    o1 = acc_ref[0, :DA_V_DIM, :] / acc_ref[0, DA_V_DIM:DA_V_DIM + 1, :]
    o2 = acc_ref[1, :DA_V_DIM, :] / acc_ref[1, DA_V_DIM:DA_V_DIM + 1, :]
    a = o1 - lam * o2
    y = a * lax.rsqrt(jnp.mean(a * a, axis=0, keepdims=True) + EPS) * g_ref[...]
    o_ref[...] = (y * (1.0 - lam_init)).astype(BF16)


def _diff_attn(proj, lq1, lk1, lq2, lk2, g, lam_init, batch, seq, layer):
    n = proj.shape[0]
    nq = seq // TQ
    kcol, vcol = COL_K_DA // LANES, COL_V_DA // LANES
    return pl.pallas_call(
        functools.partial(_diff_attn_kernel, lam_init=lam_init),
        grid=(batch, DA_HEADS, nq),
        in_specs=[
            pl.BlockSpec((TQ, LANES), lambda b, h, i: (b * nq + i, h)),
            pl.BlockSpec((TQ, LANES), lambda b, h, i: (b * nq + jnp.minimum(i + 1, nq - 1), h)),
            pl.BlockSpec((seq, LANES), lambda b, h, i: (b, kcol + h)),
            pl.BlockSpec((seq, LANES), lambda b, h, i: (b, vcol + h)),
            _layer_spec(lq1, layer), _layer_spec(lk1, layer), _layer_spec(lq2, layer), _layer_spec(lk2, layer),
            _layer_spec(g, layer),
        ],
        out_specs=pl.BlockSpec((LANES, TQ), lambda b, h, i: (h, b * nq + i)),
        out_shape=jax.ShapeDtypeStruct((DA_WIDTH, n), BF16),
        scratch_shapes=[
            pltpu.VMEM((seq // TK, VT_ROWS, TK), BF16),
            pltpu.VMEM((2, LANES, TQ), BF16),
            pltpu.VMEM((2, LANES, TQ), BF16),
            pltpu.VMEM((2 * TQ // SUB, TK, SUB), F32),
            pltpu.VMEM((2 * TQ // SUB, TK, SUB), F32),
            pltpu.VMEM((2 * TQ // SUB, 1, SUB), F32),
            pltpu.VMEM((2 * TQ // SUB, 1, SUB), F32),
            pltpu.VMEM((2, 1, TQ), F32),
            pltpu.VMEM((2, VT_ROWS, TQ), F32),
        ],
        compiler_params=_params("arbitrary", "arbitrary", "arbitrary"),
        name="diff_attn",
    )(proj, proj, proj, proj, lq1, lk1, lq2, lk2, g)


def _split_dot(x, w):
    hi = x.astype(BF16)
    lo = (x - hi.astype(F32)).astype(BF16)
    return (jnp.dot(hi, w, preferred_element_type=F32) + jnp.dot(lo, w, preferred_element_type=F32))


def _ret_pool_body(seq_start, t0, qk, v, g, u, decay_ref, xi_ref, zeta_ref, cd_ref, bd_ref,
                   avg_ref, retg_ref, win_ref, pw_ref, ps_ref, o_ref, state_ref, halo_ref, oret_ref, *, fillers):
    rows = qk.shape[0]

    def fill():
        if fillers:
            fillers.pop(0)()

    @pl.when(seq_start)
    def _():
        state_ref[...] = jnp.zeros(state_ref.shape, F32)
        halo_ref[...] = jnp.zeros(halo_ref.shape, F32)

    C = RET_CHUNK
    chunks = range(rows // C)
    qlane = lax.broadcasted_iota(jnp.int32, (C, LANES), 1) // RET_QK_DIM
    vlane = lax.broadcasted_iota(jnp.int32, (C, RET_WIDTH), 1) // RET_V_DIM
    qs = [qk[c * C:(c + 1) * C, :LANES] for c in chunks]
    ks = [qk[c * C:(c + 1) * C, LANES:] for c in chunks]
    vs = [v[c * C:(c + 1) * C, :] for c in chunks]
    inner = [[lax.dot_general(jnp.where(qlane == h, qs[c], jnp.zeros_like(qs[c])), ks[c],
                              (((1,), (1,)), ((), ())), preferred_element_type=F32)
              for h in range(RET_HEADS)] for c in chunks]
    upd = [lax.dot_general((ks[c].astype(F32) * zeta_ref[...]).astype(BF16), vs[c],
                           (((0,), (0,)), ((), ())), preferred_element_type=F32) for c in chunks]
    fill()
    states = [state_ref[...]]
    for c in chunks:
        states.append(states[c] * cd_ref[...] + bd_ref[...] * upd[c])
    state_ref[...] = states[-1]
    for c in chunks:
        o_c = jnp.dot((qs[c].astype(F32) * xi_ref[...]).astype(BF16), states[c].astype(BF16),
                      preferred_element_type=F32)
        for h in range(RET_HEADS):
            vh = jnp.where(vlane == h, vs[c], jnp.zeros_like(vs[c]))
            o_c = o_c + jnp.dot((inner[c][h] * decay_ref[h]).astype(BF16), vh, preferred_element_type=F32)
        oret_ref[c * C:(c + 1) * C, :] = o_c
        if c % 2 == 1:
            fill()

    o = oret_ref[...]
    mu = _split_dot(o, avg_ref[...])
    d = o - mu
    var = jnp.dot((d * d).astype(BF16), avg_ref[...], preferred_element_type=F32)
    y = d * lax.rsqrt(var + EPS) * retg_ref[...]
    gate = g.astype(F32)
    o_ref[:, :RET_WIDTH] = (gate * jax.nn.sigmoid(gate) * y).astype(BF16)
    fill()

    u = u.astype(F32)
    halo_groups = POOL_HALO // SUBLANES
    ext = jnp.concatenate([halo_ref[...], u], axis=0).reshape(-1, SUBLANES, POOL_WIDTH)
    halo_ref[...] = u[rows - POOL_HALO:, :]

    def shift_down(x, k):
        padded = jnp.concatenate([jnp.zeros_like(x[:1]), x], axis=0)
        if k == SUBLANES:
            return padded[:-1]
        r = pltpu.roll(padded, k, 1)
        return jnp.where(lax.broadcasted_iota(jnp.int32, x.shape, 1) < k, r[:-1], r[1:])

    assert POOL_WINDOWS == (2, 4, 8, 16) and POOL_DIM * 2 == LANES
    s2 = ext + shift_down(ext, 1)
    s4 = s2 + shift_down(s2, 2)
    s4_hi = s4[:, :, LANES:]
    s8 = s4_hi + shift_down(s4_hi, 4)
    s16 = s8 + shift_down(s8, 8)
    first = lax.broadcasted_iota(jnp.int32, s8.shape, 2) < POOL_DIM
    psum = jnp.concatenate([jnp.where(first, s2[:, :, :LANES], s4[:, :, :LANES]), jnp.where(first, s8, s16)],
                           axis=2)[halo_groups:].reshape(rows, POOL_WIDTH)
    t = (t0 + lax.broadcasted_iota(jnp.int32, u.shape, 0)).astype(F32)
    pooled = psum / jnp.minimum(t + 1.0, win_ref[...]) - u
    yp = jnp.dot(pooled.astype(BF16), pw_ref[...], preferred_element_type=F32) * ps_ref[...]
    o_ref[:, RET_WIDTH:] = yp.astype(BF16)


def _mix_up_kernel(oda_ref, orp_ref, x_ref, wo_ref, gpost_ref, gpre_ref, w_ref, cw_ref, cb_ref,
                   xo_ref, a_ref, carry_ref, ext_ref, *, tiles_per_seq):
    i = pl.program_id(0)
    tm = x_ref.shape[0]

    @pl.when(i % tiles_per_seq == 0)
    def _():
        carry_ref[...] = jnp.zeros(carry_ref.shape, F32)

    mix = (lax.dot_general(oda_ref[...], wo_ref[:DA_WIDTH, :], (((0,), (0,)), ((), ())),
                           preferred_element_type=F32)
           + jnp.dot(orp_ref[...], wo_ref[DA_WIDTH:, :], preferred_element_type=F32))
    xn = x_ref[...] + _rms(mix, gpost_ref[...])
    xo_ref[...] = xn
    h = _rms(xn, gpre_ref[...]).astype(BF16)

    def conv_cols(slot, c0):
        u = jnp.dot(h, w_ref[:, c0:c0 + MXU_COLS], preferred_element_type=F32)
        ext_ref[slot, :SUBLANES, :] = carry_ref[:, c0:c0 + MXU_COLS]
        ext_ref[slot, SUBLANES:, :] = u
        carry_ref[:, c0:c0 + MXU_COLS] = u[tm - SUBLANES:, :]
        cw = cw_ref[:, c0:c0 + MXU_COLS]
        return (cb_ref[:, c0:c0 + MXU_COLS]
                + cw[0:1] * ext_ref[slot, SUBLANES - 2:SUBLANES - 2 + tm, :]
                + cw[1:2] * ext_ref[slot, SUBLANES - 1:SUBLANES - 1 + tm, :]
                + cw[2:3] * u)

    for c in range(D_FF // MXU_COLS):
        c0 = c * MXU_COLS
        g = conv_cols(0, c0)
        v = conv_cols(1, D_FF + c0)
        th = jnp.tanh(g * (g * g * (GELU_C * 0.044715) + GELU_C))
        a_ref[:, c0:c0 + MXU_COLS] = (g * v * (th + 1.0)).astype(BF16)


def _mix_up(o_da, o_rp, x, w_out, g_post, g_pre, w_up, cw, cb, seq, layer):
    n = x.shape[0]
    row = lambda width: pl.BlockSpec((TM_MIX, width), lambda i: (i, 0))
    resident = functools.partial(_resident_spec, layer=layer)
    return pl.pallas_call(
        functools.partial(_mix_up_kernel, tiles_per_seq=seq // TM_MIX),
        grid=(n // TM_MIX,),
        in_specs=[pl.BlockSpec((DA_WIDTH, TM_MIX), lambda i: (0, i)), row(RET_WIDTH + POOL_WIDTH), row(D_MODEL),
                  resident(w_out), _layer_spec(g_post, layer), _layer_spec(g_pre, layer),
                  resident(w_up), _layer_spec(cw, layer), _layer_spec(cb, layer)],
        out_specs=[row(D_MODEL), row(D_FF)],
        out_shape=[jax.ShapeDtypeStruct((n, D_MODEL), F32), jax.ShapeDtypeStruct((n, D_FF), BF16)],
        scratch_shapes=[
            pltpu.VMEM((SUBLANES, 2 * D_FF), F32),
            pltpu.VMEM((2, TM_MIX + SUBLANES, MXU_COLS), F32),
        ],
        compiler_params=_params("arbitrary"),
        name="mix_up",
    )(o_da, o_rp, x, w_out, g_post, g_pre, w_up, cw, cb)


def _down_proj_kernel(a_ref, x_ref, w_ref, g_ref, xo_ref, wb_ref):
    @pl.when(pl.program_id(0) == 0)
    def _():
        wb_ref[...] = w_ref[...].astype(BF16)

    y = jnp.dot(a_ref[...], wb_ref[...], preferred_element_type=F32)
    xo_ref[...] = x_ref[...] + _rms(y, g_ref[...])


def _down_proj(a, x, w, g, layer):
    n = x.shape[0]
    row = lambda width: pl.BlockSpec((TM_PROJ, width), lambda i: (i, 0))
    return pl.pallas_call(
        _down_proj_kernel,
        grid=(n // TM_PROJ,),
        in_specs=[row(D_FF), row(D_MODEL), _resident_spec(w, layer), _layer_spec(g, layer)],
        out_specs=row(D_MODEL),
        out_shape=jax.ShapeDtypeStruct((n, D_MODEL), F32),
        scratch_shapes=[pltpu.VMEM((D_FF, D_MODEL), BF16)],
        compiler_params=_params("arbitrary"),
        name="down_proj",
    )(a, x, w, g)


def _rotary_tables(seq, rot_dim, theta, period, tile):
    half = rot_dim // 2
    d = np.arange(LANES) % period
    rotated = d < rot_dim
    inv = np.float64(theta) ** (-np.arange(0, rot_dim, 2, dtype=np.float64) / rot_dim)
    inv_lane = np.where(rotated, inv[np.where(rotated, d % half, 0)], 0.0)
    start = (np.arange(seq // tile, dtype=np.float64) * tile)[:, None] * inv_lane[None, :]
    row = np.arange(tile, dtype=np.float64)[:, None] * inv_lane[None, :]
    masks = np.stack([(rotated & (d >= half)).astype(np.float32), -(rotated & (d < half)).astype(np.float32)])
    return (jnp.asarray(np.stack([np.cos(start), np.sin(start)]), F32),
            jnp.asarray(np.stack([np.cos(row), np.sin(row)]), F32), jnp.asarray(masks))


def _retention_tables():
    H, C = RET_HEADS, RET_CHUNK
    f32 = np.float32
    log_g = np.log(f32(1.0) - f32(2.0) ** (f32(-5.0) - np.arange(H, dtype=f32)))
    idx = np.arange(C, dtype=f32)
    diff = idx[:, None] - idx[None, :]
    decay = np.where(diff >= 0, np.exp(np.maximum(diff, f32(0.0)) * log_g[:, None, None]), f32(0.0))
    xi = np.exp((idx + f32(1.0)) * log_g[:, None])
    zeta = np.exp((f32(C) - f32(1.0) - idx) * log_g[:, None])
    chunk_decay = np.exp(f32(C) * log_g)
    xi_t = np.repeat(xi.T, RET_QK_DIM, axis=1)
    zeta_t = np.repeat(zeta.T, RET_QK_DIM, axis=1)
    row_head = np.arange(LANES) // RET_QK_DIM
    col_head = np.arange(RET_WIDTH) // RET_V_DIM
    bd = (row_head[:, None] == col_head[None, :]).astype(f32)
    cd = bd * chunk_decay[row_head][:, None]
    avg = (col_head[:, None] == col_head[None, :]).astype(f32) / f32(RET_V_DIM)
    win = np.repeat(np.asarray(POOL_WINDOWS, f32), POOL_DIM)[None, :]
    tabs = [jnp.asarray(t.astype(f32)) for t in (decay, xi_t, zeta_t, cd, bd)]
    return (*tabs, jnp.asarray(avg, BF16), jnp.asarray(win))


def _block_diag(w):
    L, G, P, _ = w.shape
    eye = jnp.eye(G, dtype=w.dtype)
    return (eye[None, :, None, :, None] * w[:, :, :, None, :]).reshape(L, G * P, G * P)


def kernel(x, norm_mix_pre, norm_mix_post, w_in, lambda_q1, lambda_k1, lambda_q2, lambda_k2, diff_subln,
           ret_norm, pool_w, pool_scale, w_out, norm_mlp_pre, norm_mlp_post, w_up, conv_w, conv_b, w_down):
    batch, seq, _ = x.shape
    depth = w_in.shape[0]
    assert seq % TQ == 0 and seq % TM_PROJ == 0 and seq % TM_MIX == 0
    xf = x.reshape(batch * seq, D_MODEL)

    tda = _rotary_tables(seq, DA_ROT_DIM, ROPE_THETA, DA_QK_DIM, ROT_BLOCK)
    tret = _rotary_tables(seq, RET_QK_DIM, RET_THETA, RET_QK_DIM, ROT_BLOCK)
    rtabs = _retention_tables()

    rows = lambda a: a.reshape(depth, 1, -1)
    w_out_b, w_up_b = w_out.astype(BF16), w_up.astype(BF16)
    pool_w_b = _block_diag(pool_w).astype(BF16)
    half_val = jnp.concatenate([jnp.ones((D_FF,), F32), jnp.full((D_FF,), 0.5, F32)])
    conv_w_s, conv_b_s = conv_w * half_val, rows(conv_b * half_val)
    g_mix_pre, g_mix_post, g_mlp_pre, g_mlp_post = (rows(g) for g in (norm_mix_pre, norm_mix_post,
                                                                      norm_mlp_pre, norm_mlp_post))
    lq1, lk1, lq2, lk2 = (rows(v) for v in (lambda_q1, lambda_k1, lambda_q2, lambda_k2))
    g_sub = diff_subln.reshape(depth, DA_V_DIM, 1)
    g_ret, p_scale = rows(ret_norm), rows(pool_scale)

    for l in range(depth):
        lam_init = 0.8 - 0.6 * math.exp(-0.3 * l)
        proj, o_rp = _in_mix(xf, g_mix_pre, w_in, tda, tret, rtabs, g_ret, pool_w_b, p_scale, seq, l)
        o_da = _diff_attn(proj, lq1, lk1, lq2, lk2, g_sub, lam_init, batch, seq, l)
        xf, a = _mix_up(o_da, o_rp, xf, w_out_b, g_mix_post, g_mlp_pre, w_up_b, conv_w_s, conv_b_s, seq, l)
        xf = _down_proj(a, xf, w_down, g_mlp_post, l)
    return xf.reshape(batch, seq, D_MODEL)
```
